```python
import jax, jax.numpy as jnp
from jax import lax
import numpy as np

D_MODEL = 1024
BATCH = 4
SEQ = 8192
DEPTH = 1

HEAD_DIM = 64
ATTN_WIDTH = D_MODEL // 2
N_ATTN_HEADS = ATTN_WIDTH // HEAD_DIM
CONV_WIDTH = D_MODEL - ATTN_WIDTH
IN_PROJ_WIDTH = 3 * ATTN_WIDTH + 2 * CONV_WIDTH
DILATED_BRANCHES = ((128, 1), (512, 4), (2048, 16))
ATTN_BLOCK = 128
ROPE_THETA = 500000.0
ROPE_DIM = HEAD_DIM // 4
CONV_KERNEL = 31
N_EXPERTS = 256
TOP_K = 8
N_GROUPS = 8
TOPK_GROUPS = 4
EXPERT_DIM = D_MODEL // 4
SHARED_DIM = D_MODEL // 4
ROUTED_SCALE = 2.5
MOE_BLOCK = 128
NORM_EPS = 1e-6
NEG_INF = -1e30

kernel_name = "hymba_dilated_attn_conformer_conv_moe_adaln"


def rms_norm(x, g):
    xf = x.astype(jnp.float32)
    y = xf * lax.rsqrt(jnp.mean(xf * xf, axis=-1, keepdims=True) + NORM_EPS)
    return (y * g.astype(jnp.float32)).astype(x.dtype)


def layer_norm(x, g, b):
    xf = x.astype(jnp.float32)
    mu = jnp.mean(xf, axis=-1, keepdims=True)
    var = jnp.mean(jnp.square(xf - mu), axis=-1, keepdims=True)
    y = (xf - mu) * lax.rsqrt(var + NORM_EPS)
    return (y * g.astype(jnp.float32) + b.astype(jnp.float32)).astype(x.dtype)


def partial_rotary(t, positions):
    half = ROPE_DIM // 2
    inv_freq = ROPE_THETA ** (-(jnp.arange(half, dtype=jnp.float32) * 2.0) / ROPE_DIM)
    ang = positions.astype(jnp.float32)[..., None] * inv_freq
    cos = jnp.cos(ang)[:, :, None, :]
    sin = jnp.sin(ang)[:, :, None, :]
    t1 = t[..., :half]
    t2 = t[..., half:ROPE_DIM]
    return jnp.concatenate([t1 * cos - t2 * sin, t2 * cos + t1 * sin, t[..., ROPE_DIM:]], axis=-1)


def dilated_window_attention(q, k, v, window, dilation):
    n_back = window // dilation
    bsz, seq, nh, hd = q.shape
    chunk = dilation * ATTN_BLOCK
    lp = -(-seq // chunk) * chunk
    nb = lp // chunk
    pad = ((0, 0), (0, lp - seq), (0, 0), (0, 0))

    def blocks(t):
        return jnp.pad(t, pad).reshape(bsz, nb, ATTN_BLOCK, dilation, nh, hd)

    def with_prev(t):
        prev = jnp.pad(t[:, :-1], ((0, 0), (1, 0), (0, 0), (0, 0), (0, 0), (0, 0)))
        return jnp.concatenate([prev, t], axis=2)

    qb = blocks(q)
    kc = with_prev(blocks(k))
    vc = with_prev(blocks(v))
    s = jnp.einsum("bnqrhd,bnkrhd->bnrhqk", qb, kc) * (hd ** -0.5)
    qi = jnp.arange(ATTN_BLOCK)[:, None]
    kj = jnp.arange(2 * ATTN_BLOCK)[None, :]
    dist = ATTN_BLOCK + qi - kj
    band = (dist >= 0) & (dist <= n_back)
    first = (jnp.arange(nb) == 0)[:, None, None]
    valid = band[None] & ~(first & (kj < ATTN_BLOCK)[None])
    s = jnp.where(valid[None, :, None, None], s, NEG_INF)
    lse = jax.nn.logsumexp(s, axis=-1)
    p = jnp.exp(s - lse[..., None])
    o = jnp.einsum("bnrhqk,bnkrhd->bnqrhd", p, vc).reshape(bsz, lp, nh, hd)[:, :seq]
    lse = lse.transpose(0, 1, 4, 2, 3).reshape(bsz, lp, nh)[:, :seq]
    return o, lse


def conformer_conv(u, g, w_dw, b_dw, ln_g, ln_b, w_pw2, b_pw2):
    a = u * jax.nn.sigmoid(g)
    y = lax.conv_general_dilated(
        a, w_dw[:, None, :].astype(a.dtype), window_strides=(1,),
        padding=[(CONV_KERNEL - 1, 0)],
        dimension_numbers=("NWC", "WIO", "NWC"),
        feature_group_count=a.shape[-1]) + b_dw
    y = jax.nn.silu(layer_norm(y, ln_g, ln_b))
    return y @ w_pw2 + b_pw2


def moe_ffn(h, w_router, b_router, w_g, w_u, w_d, ws_g, ws_u, ws_d):
    T, D = h.shape
    scores = jax.nn.sigmoid(h.astype(jnp.float32) @ w_router.astype(jnp.float32))
    biased = scores + b_router.astype(jnp.float32)
    grp = biased.reshape(T, N_GROUPS, N_EXPERTS // N_GROUPS)
    grp_score = lax.top_k(grp, 2)[0].sum(-1)
    _, top_groups = lax.top_k(grp_score, TOPK_GROUPS)
    grp_mask = jax.nn.one_hot(top_groups, N_GROUPS, dtype=jnp.float32).sum(1) > 0
    exp_mask = jnp.repeat(grp_mask, N_EXPERTS // N_GROUPS, axis=1)
    _, idx = lax.top_k(jnp.where(exp_mask, biased, NEG_INF), TOP_K)
    wts = jnp.take_along_axis(scores, idx, axis=1)
    wts = wts / jnp.sum(wts, axis=-1, keepdims=True) * ROUTED_SCALE

    A = T * TOP_K
    e_flat = idx.reshape(-1)
    tok_flat = jnp.repeat(jnp.arange(T, dtype=jnp.int32), TOP_K)
    w_flat = wts.reshape(-1)
    order = jnp.argsort(e_flat)
    e_sorted = e_flat[order]
    counts = jnp.bincount(e_flat, length=N_EXPERTS)
    padded = ((counts + MOE_BLOCK - 1) // MOE_BLOCK) * MOE_BLOCK
    pend = jnp.cumsum(padded)
    pstart = pend - padded
    start = jnp.cumsum(counts) - counts
    dest = pstart[e_sorted] + jnp.arange(A) - start[e_sorted]
    nblk = -(-A // MOE_BLOCK) + N_EXPERTS
    rows_tok = jnp.full((nblk * MOE_BLOCK,), T, jnp.int32).at[dest].set(tok_flat[order])
    rows_w = jnp.zeros((nblk * MOE_BLOCK,), jnp.float32).at[dest].set(w_flat[order])
    block_e = jnp.clip(jnp.searchsorted(pend, jnp.arange(nblk) * MOE_BLOCK, side="right"),
                       0, N_EXPERTS - 1)
    h_pad = jnp.concatenate([h, jnp.zeros((1, D), h.dtype)], axis=0)

    def expert_block(acc, blk):
        rt, rw, e = blk
        xb = h_pad[rt]
        a = jax.nn.silu(xb @ w_g[e]) * (xb @ w_u[e])
        yb = (a @ w_d[e]).astype(jnp.float32) * rw[:, None]
        return acc.at[rt].add(yb), None

    acc, _ = lax.scan(expert_block, jnp.zeros((T + 1, D), jnp.float32),
                      (rows_tok.reshape(nblk, MOE_BLOCK), rows_w.reshape(nblk, MOE_BLOCK), block_e))
    routed = acc[:T].astype(h.dtype)
    shared = (jax.nn.silu(h @ ws_g) * (h @ ws_u)) @ ws_d
    return routed + shared


def setup_inputs(seed: int = 0) -> dict:
    key = jax.random.key(seed)
    ks = jax.random.split(key, 32)
    f32 = jnp.float32
    D, L, A, C, E, F = D_MODEL, DEPTH, ATTN_WIDTH, CONV_WIDTH, N_EXPERTS, EXPERT_DIM

    def nrm(k, shape, scale):
        return jax.random.normal(k, shape, f32) * scale

    def gain(k, shape):
        return 1.0 + 0.02 * jax.random.normal(k, shape, f32)

    offs = jax.random.randint(ks[2], (BATCH, 1), 0, 4096)
    positions = (offs + jnp.arange(SEQ, dtype=jnp.int32)[None, :]).astype(jnp.int32)
    return {
        "x": nrm(ks[0], (BATCH, SEQ, D), 1.0),
        "c": nrm(ks[1], (BATCH, D), 1.0),
        "positions": positions,
        "g_norm1": gain(ks[3], (L, D)),
        "w_ada": nrm(ks[4], (L, D, 6 * D), 0.5 * D ** -0.5),
        "b_ada": nrm(ks[5], (L, 6 * D), 0.02),
        "w_in": nrm(ks[6], (L, D, IN_PROJ_WIDTH), D ** -0.5),
        "w_dw": nrm(ks[7], (L, CONV_KERNEL, C), CONV_KERNEL ** -0.5),
        "b_dw": nrm(ks[8], (L, C), 0.02),
        "g_conv_ln": gain(ks[9], (L, C)),
        "b_conv_ln": nrm(ks[10], (L, C), 0.02),
        "w_pw2": nrm(ks[11], (L, C, C), C ** -0.5),
        "b_pw2": nrm(ks[12], (L, C), 0.02),
        "g_attn_out": gain(ks[13], (L, A)),
        "g_conv_out": gain(ks[14], (L, C)),
        "w_out": nrm(ks[15], (L, A + C, D), (A + C) ** -0.5),
        "g_norm2": gain(ks[16], (L, D)),
        "w_router": nrm(ks[17], (L, D, E), D ** -0.5),
        "b_router": nrm(ks[18], (L, E), 0.01),
        "w_exp_gate": nrm(ks[19], (L, E, D, F), D ** -0.5),
        "w_exp_up": nrm(ks[20], (L, E, D, F), D ** -0.5),
        "w_exp_down": nrm(ks[21], (L, E, F, D), F ** -0.5),
        "w_sh_gate": nrm(ks[22], (L, D, SHARED_DIM), D ** -0.5),
        "w_sh_up": nrm(ks[23], (L, D, SHARED_DIM), D ** -0.5),
        "w_sh_down": nrm(ks[24], (L, SHARED_DIM, D), SHARED_DIM ** -0.5),
        "g_final": gain(ks[25], (D,)),
    }


def reference(x, c, positions, g_norm1, w_ada, b_ada, w_in, w_dw, b_dw, g_conv_ln, b_conv_ln,
              w_pw2, b_pw2, g_attn_out, g_conv_out, w_out, g_norm2, w_router, b_router,
              w_exp_gate, w_exp_up, w_exp_down, w_sh_gate, w_sh_up, w_sh_down, g_final):
    bsz, seq, d = x.shape
    A, C = ATTN_WIDTH, CONV_WIDTH
    for l in range(DEPTH):
        ada = (jax.nn.silu(c) @ w_ada[l] + b_ada[l])[:, None, :]
        sh1, sc1, gt1, sh2, sc2, gt2 = jnp.split(ada, 6, axis=-1)

        h = rms_norm(x, g_norm1[l]) * (1.0 + sc1) + sh1
        proj = h @ w_in[l]
        q, k, v, cu, cg = jnp.split(proj, [A, 2 * A, 3 * A, 3 * A + C], axis=-1)
        heads = lambda t: t.reshape(bsz, seq, N_ATTN_HEADS, HEAD_DIM).astype(jnp.float32)
        qh = partial_rotary(heads(q), positions)
        kh = partial_rotary(heads(k), positions)
        vh = heads(v)
        branches = [dilated_window_attention(qh, kh, vh, w, dl) for (w, dl) in DILATED_BRANCHES]
        o_all = jnp.stack([b[0] for b in branches], axis=0)
        alpha = jax.nn.softmax(jnp.stack([b[1] for b in branches], axis=0), axis=0)
        attn = jnp.sum(alpha[..., None] * o_all, axis=0).reshape(bsz, seq, A).astype(x.dtype)

        conv = conformer_conv(cu, cg, w_dw[l], b_dw[l], g_conv_ln[l], b_conv_ln[l],
                              w_pw2[l], b_pw2[l])

        mixed = jnp.concatenate([rms_norm(attn, g_attn_out[l]), rms_norm(conv, g_conv_out[l])],
                                axis=-1) @ w_out[l]
        x = x + gt1 * mixed

        h2 = rms_norm(x, g_norm2[l]) * (1.0 + sc2) + sh2
        y = moe_ffn(h2.reshape(bsz * seq, d), w_router[l], b_router[l], w_exp_gate[l],
                    w_exp_up[l], w_exp_down[l], w_sh_gate[l], w_sh_up[l], w_sh_down[l])
        x = x + gt2 * y.reshape(bsz, seq, d)
    return rms_norm(x, g_final)
```

```python
import functools

import jax
import jax.numpy as jnp
from jax import lax
from jax.experimental import pallas as pl
from jax.experimental.pallas import tpu as pltpu

F32 = jnp.float32
BF16 = jnp.bfloat16
I32 = jnp.int32

HEAD_DIM = 64
LANES = 128
ROPE_DIM = HEAD_DIM // 4
ROPE_THETA = 500000.0
DILATED_BRANCHES = ((128, 1), (512, 4), (2048, 16))
ATTN_BLOCK = 128
ATTN_CHUNK = 2048
CONV_KERNEL = 31
CONV_HALO = 32
N_EXPERTS = 256
TOP_K = 8
N_GROUPS = 8
TOPK_GROUPS = 4
ROUTED_SCALE = 2.5
NORM_EPS = 1e-6
NEG_INF = -1e30
NEG_BIG = -3e38

TM_INPROJ = 512
TM_MIX = 256
TM_ROUTER = 256
MOE_ROWS = 256
TM_COMBINE = 256
VMEM_LIMIT = 56 * 1024 * 1024


def _params(*sem):
    return pltpu.CompilerParams(dimension_semantics=sem, vmem_limit_bytes=VMEM_LIMIT)


def _rms(x, g):
    return x * lax.rsqrt(jnp.mean(x * x, axis=-1, keepdims=True) + NORM_EPS) * g


def _ada_kernel(c_ref, w_ref, b_ref, o_ref):
    s = c_ref[...]
    s = s * jax.nn.sigmoid(s)
    o_ref[...] = jnp.dot(s, w_ref[...], precision=lax.Precision.HIGHEST,
                         preferred_element_type=F32) + b_ref[...]


def _ada(c, w_ada, b_ada):
    bsz, d = c.shape
    n = w_ada.shape[1]
    rows = 8
    cp = jnp.pad(c, ((0, rows - bsz), (0, 0)))
    tn = 1024
    out = pl.pallas_call(
        _ada_kernel,
        grid=(n // tn,),
        in_specs=[pl.BlockSpec((rows, d), lambda j: (0, 0)),
                  pl.BlockSpec((d, tn), lambda j: (0, j)),
                  pl.BlockSpec((1, tn), lambda j: (0, j))],
        out_specs=pl.BlockSpec((rows, tn), lambda j: (0, j)),
        out_shape=jax.ShapeDtypeStruct((rows, n), F32),
        compiler_params=_params("arbitrary"),
        name="ada",
    )(cp, w_ada, b_ada.reshape(1, n))
    return out[:bsz]


def _rope_kernel(pos_ref, f_ref, cos_ref, sin_ref):
    ang = pos_ref[...].astype(F32) * f_ref[...]
    cos_ref[...] = jnp.cos(ang)
    sin_ref[...] = jnp.sin(ang)


def _rope_tables(positions):
    half = ROPE_DIM // 2
    t = positions.size
    per_row = LANES // half
    inv_freq = ROPE_THETA ** (-(jnp.arange(half, dtype=F32) * 2.0) / ROPE_DIM)
    pos_rep = jnp.repeat(positions.reshape(t // per_row, per_row), half, axis=1)
    f_lane = jnp.tile(inv_freq, per_row).reshape(1, LANES)
    cos_t, sin_t = pl.pallas_call(
        _rope_kernel,
        out_shape=(jax.ShapeDtypeStruct((t // per_row, LANES), F32),) * 2,
        name="rope",
    )(pos_rep, f_lane)
    cos8 = cos_t.reshape(t, half)
    sin8 = sin_t.reshape(t, half)
    rest = HEAD_DIM - ROPE_DIM
    cos_h = jnp.concatenate([cos8, cos8, jnp.ones((t, rest), F32)], axis=1)
    sin_h = jnp.concatenate([sin8, sin8, jnp.zeros((t, rest), F32)], axis=1)
    return jnp.tile(cos_h, (1, LANES // HEAD_DIM)), jnp.tile(sin_h, (1, LANES // HEAD_DIM))


def _inproj_kernel(x_ref, sc_ref, sh_ref, g_ref, w_ref, cos_ref, sin_ref,
                   q_ref, k_ref, v_ref, a_ref, *, attn_w, conv_w):
    x = x_ref[0]
    h = _rms(x, g_ref[...]) * (1.0 + sc_ref[0]) + sh_ref[0]
    proj = jnp.dot(h.astype(BF16), w_ref[...], preferred_element_type=F32)
    cosv = cos_ref[...]
    sinv = sin_ref[...]
    lane = lax.broadcasted_iota(I32, cosv.shape, 1)
    first_half = (lane & (HEAD_DIM - 1)) < (ROPE_DIM // 2)

    def rope(t):
        up = pltpu.roll(t, LANES - ROPE_DIM // 2, 1)
        dn = pltpu.roll(t, ROPE_DIM // 2, 1)
        return t * cosv + jnp.where(first_half, -up, dn) * sinv

    scale = HEAD_DIM ** -0.5
    for hp in range(attn_w // LANES):
        lo = hp * LANES
        q_ref[0, hp] = (rope(proj[:, lo:lo + LANES]) * scale).astype(BF16)
        k_ref[0, hp] = rope(proj[:, attn_w + lo:attn_w + lo + LANES]).astype(BF16)
        v_ref[0, hp] = proj[:, 2 * attn_w + lo:2 * attn_w + lo + LANES].astype(BF16)
    cu = proj[:, 3 * attn_w:3 * attn_w + conv_w]
    cg = proj[:, 3 * attn_w + conv_w:]
    a_ref[0] = cu * jax.nn.sigmoid(cg)


def _inproj(x, sc1, sh1, g1, w_in_bf16, cos_l, sin_l, attn_w, conv_w):
    bsz, seq, d = x.shape
    tm = min(TM_INPROJ, seq)
    n = w_in_bf16.shape[1]
    nhp = attn_w // LANES
    nt = seq // tm
    qkv_shape = jax.ShapeDtypeStruct((bsz, nhp, seq, LANES), BF16)
    qkv_spec = pl.BlockSpec((1, nhp, tm, LANES), lambda b, i: (b, 0, i, 0))
    vec_spec = pl.BlockSpec((1, 1, d), lambda b, i: (b, 0, 0))
    return pl.pallas_call(
        functools.partial(_inproj_kernel, attn_w=attn_w, conv_w=conv_w),
        grid=(bsz, nt),
        in_specs=[pl.BlockSpec((1, tm, d), lambda b, i: (b, i, 0)),
                  vec_spec, vec_spec,
                  pl.BlockSpec((1, d), lambda b, i: (0, 0)),
                  pl.BlockSpec((d, n), lambda b, i: (0, 0)),
                  pl.BlockSpec((tm, LANES), lambda b, i: (b * nt + i, 0)),
                  pl.BlockSpec((tm, LANES), lambda b, i: (b * nt + i, 0))],
        out_specs=[qkv_spec, qkv_spec, qkv_spec,
                   pl.BlockSpec((1, tm, conv_w), lambda b, i: (b, i, 0))],
        out_shape=[qkv_shape, qkv_shape, qkv_shape,
                   jax.ShapeDtypeStruct((bsz, seq, conv_w), F32)],
        compiler_params=_params("arbitrary", "arbitrary"),
        name="inproj",
    )(x, sc1, sh1, g1, w_in_bf16, cos_l, sin_l)


def _attn_kernel(q_ref, k_ref, v_ref, kp_ref, vp_ref, o_ref, l_ref, bias_ref, *, dil, n_back):
    blk = ATTN_BLOCK
    nblk = q_ref.shape[2] // blk
    qi = lax.broadcasted_iota(I32, (2 * blk, 2 * blk), 0) & (blk - 1)
    kj = lax.broadcasted_iota(I32, (2 * blk, 2 * blk), 1)
    dist = blk + qi - kj
    band = (dist >= 0) & (dist <= n_back)
    kmin = jnp.where(pl.program_id(2) > 0, 0, blk)
    bias_ref[0] = jnp.where(band & (kj >= kmin), 0.0, NEG_INF)
    bias_ref[1] = jnp.where(band, 0.0, NEG_INF)
    lane = lax.broadcasted_iota(I32, (blk, LANES), 1)
    lo = lane < HEAD_DIM

    for j in range(nblk):
        rows = slice(j * blk, (j + 1) * blk)
        prows = slice((j - 1) * blk, j * blk)
        for r in range(dil):
            cols = slice(r * LANES, (r + 1) * LANES)
            qb = q_ref[0, 0, rows, cols]
            if j == 0:
                kprev = kp_ref[0, 0, :, cols]
                vprev = vp_ref[0, 0, :, cols]
            else:
                kprev = k_ref[0, 0, prows, cols]
                vprev = v_ref[0, 0, prows, cols]
            kk = jnp.concatenate([kprev, k_ref[0, 0, rows, cols]], axis=0)
            vv = jnp.concatenate([vprev, v_ref[0, 0, rows, cols]], axis=0)
            zero = jnp.zeros_like(qb)
            q2 = jnp.concatenate([jnp.where(lo, qb, zero), jnp.where(lo, zero, qb)], axis=0)
            s = lax.dot_general(q2, kk, (((1,), (1,)), ((), ())), preferred_element_type=F32)
            s = s + bias_ref[0 if j == 0 else 1]
            m = jnp.max(s, axis=-1, keepdims=True)
            p = jnp.exp(s - m)
            l = jnp.sum(p, axis=-1, keepdims=True)
            pv = jnp.dot(p.astype(BF16), vv, preferred_element_type=F32)
            o2 = pv * (1.0 / l)
            lse2 = m + jnp.log(l)
            o_ref[0, 0, rows, cols] = jnp.where(lo, o2[:blk], o2[blk:])
            l_ref[0, 0, rows, cols] = jnp.where(lo, jnp.broadcast_to(lse2[:blk], (blk, LANES)),
                                                jnp.broadcast_to(lse2[blk:], (blk, LANES)))


def _attn_branch(q4, k4, v4, window, dil):
    bsz, nhp, seq, _ = q4.shape
    n_back = window // dil
    assert n_back <= ATTN_BLOCK and ATTN_CHUNK % (dil * ATTN_BLOCK) == 0 and seq % ATTN_CHUNK == 0
    rows = ATTN_CHUNK // dil
    nblk = rows // ATTN_BLOCK
    width = dil * LANES
    view = lambda t: t.reshape(bsz, nhp, seq // dil, width)
    cur = pl.BlockSpec((1, 1, rows, width), lambda b, h, n: (b, h, n, 0))
    prev = pl.BlockSpec((1, 1, ATTN_BLOCK, width),
                        lambda b, h, n: (b, h, jnp.maximum(n * nblk - 1, 0), 0))
    out_shape = jax.ShapeDtypeStruct((bsz, nhp, seq // dil, width), F32)
    o, l = pl.pallas_call(
        functools.partial(_attn_kernel, dil=dil, n_back=n_back),
        grid=(bsz, nhp, seq // ATTN_CHUNK),
        in_specs=[cur, cur, cur, prev, prev],
        out_specs=[cur, cur],
        out_shape=[out_shape, out_shape],
        scratch_shapes=[pltpu.VMEM((2, 2 * ATTN_BLOCK, 2 * ATTN_BLOCK), F32)],
        compiler_params=_params("arbitrary", "arbitrary", "arbitrary"),
        name=f"attn_d{dil}",
    )(view(q4), view(k4), view(v4), view(k4), view(v4))
    return o.reshape(bsz, nhp, seq, LANES), l.reshape(bsz, nhp, seq, LANES)


def _mix_kernel(o1_ref, l1_ref, o4_ref, l4_ref, o16_ref, l16_ref, a_ref, ap_ref, x_ref,
                gt_ref, sc_ref, sh_ref, wdw_ref, bdw_ref, lng_ref, lnb_ref, wpw_ref, bpw_ref,
                ga_ref, gc_ref, wout_ref, g2_ref, x1_ref, h2_ref, acat_ref):
    tm = a_ref.shape[1]
    acat_ref[CONV_HALO:, :] = a_ref[0]
    halo = ap_ref[0]
    acat_ref[:CONV_HALO, :] = jnp.where(pl.program_id(1) > 0, halo, jnp.zeros_like(halo))
    off = CONV_HALO - (CONV_KERNEL - 1)
    y = acat_ref[pl.ds(off, tm), :] * wdw_ref[0:1, :]
    for j in range(1, CONV_KERNEL):
        y = y + acat_ref[pl.ds(off + j, tm), :] * wdw_ref[j:j + 1, :]
    y = y + bdw_ref[...]
    mu = jnp.mean(y, axis=-1, keepdims=True)
    yc = y - mu
    var = jnp.mean(yc * yc, axis=-1, keepdims=True)
    yn = yc * lax.rsqrt(var + NORM_EPS) * lng_ref[...] + lnb_ref[...]
    z = yn * jax.nn.sigmoid(yn)
    conv = jnp.dot(z.astype(BF16), wpw_ref[...], preferred_element_type=F32) + bpw_ref[...]
    nc = _rms(conv, gc_ref[...])
    parts = []
    for hp in range(o1_ref.shape[1]):
        l1, l4, l16 = l1_ref[0, hp], l4_ref[0, hp], l16_ref[0, hp]
        m = jnp.maximum(jnp.maximum(l1, l4), l16)
        e1, e4, e16 = jnp.exp(l1 - m), jnp.exp(l4 - m), jnp.exp(l16 - m)
        num = e1 * o1_ref[0, hp] + e4 * o4_ref[0, hp] + e16 * o16_ref[0, hp]
        parts.append(num * (1.0 / (e1 + e4 + e16)))
    attn = jnp.concatenate(parts, axis=1)
    na = _rms(attn, ga_ref[...])
    cat = jnp.concatenate([na, nc], axis=1).astype(BF16)
    mixed = jnp.dot(cat, wout_ref[...], preferred_element_type=F32)
    x1 = x_ref[0] + gt_ref[0] * mixed
    x1_ref[0] = x1
    h2_ref[...] = _rms(x1, g2_ref[...]) * (1.0 + sc_ref[0]) + sh_ref[0]


def _mix(branches, a, x, gt1, sc2, sh2, w_dw, b_dw, ln_g, ln_b, w_pw2_bf16, b_pw2,
         g_attn, g_conv, w_out_bf16, g2):
    bsz, seq, d = x.shape
    cw = a.shape[-1]
    tm = min(TM_MIX, seq)
    nt = seq // tm
    nhp = branches[0][0].shape[1]
    br_spec = pl.BlockSpec((1, nhp, tm, LANES), lambda b, i: (b, 0, i, 0))
    vec_d = pl.BlockSpec((1, 1, d), lambda b, i: (b, 0, 0))
    full = lambda arr: pl.BlockSpec(arr.shape, lambda b, i: (0,) * arr.ndim)
    row = lambda v: v.reshape(1, -1)
    halo_blocks = tm // CONV_HALO
    consts = [w_dw, row(b_dw), row(ln_g), row(ln_b), w_pw2_bf16, row(b_pw2),
              row(g_attn), row(g_conv), w_out_bf16, row(g2)]
    flat = [t for pair in branches for t in pair]
    x1, h2 = pl.pallas_call(
        _mix_kernel,
        grid=(bsz, nt),
        in_specs=[br_spec] * 6 + [
            pl.BlockSpec((1, tm, cw), lambda b, i: (b, i, 0)),
            pl.BlockSpec((1, CONV_HALO, cw),
                         lambda b, i: (b, jnp.maximum(i * halo_blocks - 1, 0), 0)),
            pl.BlockSpec((1, tm, d), lambda b, i: (b, i, 0)),
            vec_d, vec_d, vec_d] + [full(t) for t in consts],
        out_specs=[pl.BlockSpec((1, tm, d), lambda b, i: (b, i, 0)),
                   pl.BlockSpec((tm, d), lambda b, i: (b * nt + i, 0))],
        out_shape=[jax.ShapeDtypeStruct((bsz, seq, d), F32),
                   jax.ShapeDtypeStruct((bsz * seq, d), F32)],
        scratch_shapes=[pltpu.VMEM((tm + CONV_HALO, cw), F32)],
        compiler_params=_params("arbitrary", "arbitrary"),
        name="mix",
    )(*flat, a, a, x, gt1, sc2, sh2, *consts)
    return x1, h2


def _router_kernel(h_ref, w_ref, b_ref, idx_ref, wts_ref):
    h = h_ref[...]
    w = w_ref[...]
    h_hi = h.astype(BF16)
    h_lo = (h - h_hi.astype(F32)).astype(BF16)
    w_hi = w.astype(BF16)
    w_lo = (w - w_hi.astype(F32)).astype(BF16)
    nt = (((1,), (1,)), ((), ()))
    logits = (lax.dot_general(w_hi, h_hi, nt, preferred_element_type=F32)
              + lax.dot_general(w_hi, h_lo, nt, preferred_element_type=F32)
              + lax.dot_general(w_lo, h_hi, nt, preferred_element_type=F32))
    scores = jax.nn.sigmoid(logits)
    biased = scores + b_ref[...]
    ne, tm = biased.shape
    gsz = ne // N_GROUPS
    gio = lax.broadcasted_iota(I32, (gsz, tm), 0).astype(F32)
    gscore = []
    for g in range(N_GROUPS):
        blk = biased[g * gsz:(g + 1) * gsz]
        m1 = jnp.max(blk, axis=0, keepdims=True)
        i1 = jnp.min(jnp.where(blk == m1, gio, float(gsz)), axis=0, keepdims=True)
        m2 = jnp.max(jnp.where(gio == i1, NEG_BIG, blk), axis=0, keepdims=True)
        gscore.append(m1 + m2)
    masked = []
    for g in range(N_GROUPS):
        beaten = jnp.zeros_like(gscore[g])
        for o in range(N_GROUPS):
            if o == g:
                continue
            wins = (gscore[o] >= gscore[g]) if o < g else (gscore[o] > gscore[g])
            beaten = beaten + jnp.where(wins, 1.0, 0.0)
        keep = beaten < float(TOPK_GROUPS)
        masked.append(jnp.where(keep, biased[g * gsz:(g + 1) * gsz], NEG_INF))
    cur = jnp.concatenate(masked, axis=0)
    eio = lax.broadcasted_iota(I32, (ne, tm), 0).astype(F32)
    picks, vals = [], []
    for _ in range(TOP_K):
        m = jnp.max(cur, axis=0, keepdims=True)
        ix = jnp.min(jnp.where(cur == m, eio, float(ne)), axis=0, keepdims=True)
        hit = eio == ix
        vals.append(jnp.sum(jnp.where(hit, scores, 0.0), axis=0, keepdims=True))
        cur = jnp.where(hit, NEG_BIG, cur)
        picks.append(ix)
    total = vals[0]
    for v in vals[1:]:
        total = total + v
    for k in range(TOP_K):
        idx_ref[k:k + 1, :] = picks[k].astype(I32)
        wts_ref[k:k + 1, :] = vals[k] / total * ROUTED_SCALE


def _router(h2, w_router_t, b_router):
    t, d = h2.shape
    ne = w_router_t.shape[0]
    tm = min(TM_ROUTER, t)
    return pl.pallas_call(
        _router_kernel,
        grid=(t // tm,),
        in_specs=[pl.BlockSpec((tm, d), lambda i: (i, 0)),
                  pl.BlockSpec((ne, d), lambda i: (0, 0)),
                  pl.BlockSpec((ne, 1), lambda i: (0, 0))],
        out_specs=[pl.BlockSpec((TOP_K, tm), lambda i: (0, i)),
                   pl.BlockSpec((TOP_K, tm), lambda i: (0, i))],
        out_shape=[jax.ShapeDtypeStruct((TOP_K, t), I32),
                   jax.ShapeDtypeStruct((TOP_K, t), F32)],
        compiler_params=_params("arbitrary"),
        name="router",
    )(h2, w_router_t, b_router.reshape(ne, 1))


def _dispatch(idx, wts):
    k, t = idx.shape
    a = k * t
    blk = MOE_ROWS
    e_flat = idx.reshape(-1)
    tok_flat = jnp.tile(jnp.arange(t, dtype=I32), k)
    w_flat = wts.reshape(-1)
    order = jnp.argsort(e_flat)
    e_sorted = e_flat[order]
    counts = jnp.bincount(e_flat, length=N_EXPERTS).astype(I32)
    padded = ((counts + blk - 1) // blk) * blk
    pend = jnp.cumsum(padded)
    pstart = pend - padded
    start = jnp.cumsum(counts) - counts
    nblk = -(-a // blk) + N_EXPERTS
    n_used = (pend[-1] // blk).astype(I32)
    block_e = jnp.clip(jnp.searchsorted(pend, jnp.arange(nblk, dtype=I32) * blk, side="right"),
                       0, N_EXPERTS - 1).astype(I32)
    p = jnp.arange(nblk * blk, dtype=I32)
    pe = jnp.repeat(block_e, blk)
    rank = p - pstart[pe]
    valid = (rank < counts[pe]) & (p < pend[-1])
    src = jnp.clip(start[pe] + rank, 0, a - 1)
    rows_tok = jnp.where(valid, tok_flat[order][src], 0).astype(I32)
    rows_w = jnp.where(valid, w_flat[order][src], 0.0).astype(F32)
    dest_sorted = pstart[e_sorted] + jnp.arange(a, dtype=I32) - start[e_sorted]
    dest = jnp.zeros((a,), I32).at[order].set(dest_sorted.astype(I32)).reshape(k, t)
    return block_e, n_used.reshape(1), rows_tok, rows_w, dest, nblk


def _row_copy(src_hbm, row, dst, dst_row, sem):
    return pltpu.make_async_copy(src_hbm.at[pl.ds(row, 1), :], dst.at[pl.ds(dst_row, 1), :], sem)


def _experts_kernel(be_ref, nu_ref, tokc_ref, tokn_ref, rw_ref, h_hbm, wg_ref, wu_ref, wd_ref,
                    ys_ref, xbuf, sem, wgb, wub, wdb):
    i = pl.program_id(0)
    nu = nu_ref[0]
    slot = i % 2
    rows = xbuf.shape[1]

    def gather(tok_ref, s, wait):
        def body(r, carry):
            cp = _row_copy(h_hbm, tok_ref[0, 0, r], xbuf.at[s], r, sem.at[s])
            if wait:
                cp.wait()
            else:
                cp.start()
            return carry
        lax.fori_loop(0, rows, body, 0, unroll=8)

    @pl.when(i == 0)
    def _():
        gather(tokc_ref, slot, False)

    @pl.when(i + 1 < nu)
    def _():
        gather(tokn_ref, 1 - slot, False)

    prev_e = be_ref[jnp.maximum(i - 1, 0)]
    fresh = jnp.logical_or(i == 0, be_ref[i] != prev_e)

    @pl.when(jnp.logical_and(fresh, i < nu))
    def _():
        wgb[...] = wg_ref[0].astype(BF16)
        wub[...] = wu_ref[0].astype(BF16)
        wdb[...] = wd_ref[0].astype(BF16)

    @pl.when(i < nu)
    def _():
        gather(tokc_ref, slot, True)
        xb = xbuf[slot].astype(BF16)
        g = jnp.dot(xb, wgb[...], preferred_element_type=F32)
        u = jnp.dot(xb, wub[...], preferred_element_type=F32)
        act = (g * jax.nn.sigmoid(g)) * u
        y = jnp.dot(act.astype(BF16), wdb[...], preferred_element_type=F32)
        ys_ref[...] = y * rw_ref[...]

    @pl.when(i >= nu)
    def _():
        ys_ref[...] = jnp.zeros_like(ys_ref)


def _experts(h2, block_e, n_used, rows_tok, rows_w, nblk, w_g, w_u, w_d):
    t, d = h2.shape
    f = w_g.shape[-1]
    blk = MOE_ROWS
    tok3 = rows_tok.reshape(nblk, 1, blk)
    last = lambda i, be, nu: jnp.minimum(i, nu[0] - 1)
    grid_spec = pltpu.PrefetchScalarGridSpec(
        num_scalar_prefetch=2,
        grid=(nblk,),
        in_specs=[
            pl.BlockSpec((1, 1, blk), lambda i, be, nu: (last(i, be, nu), 0, 0),
                         memory_space=pltpu.SMEM),
            pl.BlockSpec((1, 1, blk), lambda i, be, nu: (jnp.minimum(i + 1, nu[0] - 1), 0, 0),
                         memory_space=pltpu.SMEM),
            pl.BlockSpec((blk, 1), lambda i, be, nu: (last(i, be, nu), 0)),
            pl.BlockSpec(memory_space=pl.ANY),
            pl.BlockSpec((1, d, f), lambda i, be, nu: (be[i], 0, 0)),
            pl.BlockSpec((1, d, f), lambda i, be, nu: (be[i], 0, 0)),
            pl.BlockSpec((1, f, d), lambda i, be, nu: (be[i], 0, 0)),
        ],
        out_specs=pl.BlockSpec((blk, d), lambda i, be, nu: (i, 0)),
        scratch_shapes=[pltpu.VMEM((2, blk, d), F32),
                        pltpu.SemaphoreType.DMA((2,)),
                        pltpu.VMEM((d, f), BF16), pltpu.VMEM((d, f), BF16),
                        pltpu.VMEM((f, d), BF16)],
    )
    return pl.pallas_call(
        _experts_kernel,
        grid_spec=grid_spec,
        out_shape=jax.ShapeDtypeStruct((nblk * blk, d), F32),
        compiler_params=_params("arbitrary"),
        name="experts",
    )(block_e, n_used, tok3, tok3, rows_w.reshape(-1, 1), h2, w_g, w_u, w_d)


def _combine_kernel(dc_ref, dn_ref, ys_hbm, h_ref, x_ref, gt_ref, wsg_ref, wsu_ref, wsd_ref,
                    gf_ref, o_ref, buf, sem):
    i = pl.program_id(0)
    n = pl.num_programs(0)
    slot = i % 2
    tm = h_ref.shape[0]

    def gather(d_ref, s, wait):
        def body(r, carry):
            for k in range(TOP_K):
                cp = _row_copy(ys_hbm, d_ref[k, r], buf.at[s, k], r, sem.at[s])
                if wait:
                    cp.wait()
                else:
                    cp.start()
            return carry
        lax.fori_loop(0, tm, body, 0, unroll=2)

    @pl.when(i == 0)
    def _():
        gather(dc_ref, slot, False)

    @pl.when(i + 1 < n)
    def _():
        gather(dn_ref, 1 - slot, False)

    hb = h_ref[...].astype(BF16)
    g = jnp.dot(hb, wsg_ref[...], preferred_element_type=F32)
    u = jnp.dot(hb, wsu_ref[...], preferred_element_type=F32)
    act = (g * jax.nn.sigmoid(g)) * u
    y = jnp.dot(act.astype(BF16), wsd_ref[...], preferred_element_type=F32)
    gather(dc_ref, slot, True)
    routed = buf[slot, 0]
    for k in range(1, TOP_K):
        routed = routed + buf[slot, k]
    xo = x_ref[...] + gt_ref[0] * (routed + y)
    o_ref[...] = _rms(xo, gf_ref[...])


def _combine(ys, dest, h2, x1_flat, gt2, ws_g, ws_u, ws_d, g_final, seq):
    t, d = h2.shape
    f = ws_g.shape[1]
    tm = min(TM_COMBINE, seq)
    nt = t // tm
    per_seq = seq // tm
    return pl.pallas_call(
        _combine_kernel,
        grid=(nt,),
        in_specs=[
            pl.BlockSpec((TOP_K, tm), lambda i: (0, i), memory_space=pltpu.SMEM),
            pl.BlockSpec((TOP_K, tm), lambda i: (0, jnp.minimum(i + 1, nt - 1)),
                         memory_space=pltpu.SMEM),
            pl.BlockSpec(memory_space=pl.ANY),
            pl.BlockSpec((tm, d), lambda i: (i, 0)),
            pl.BlockSpec((tm, d), lambda i: (i, 0)),
            pl.BlockSpec((1, 1, d), lambda i: (i // per_seq, 0, 0)),
            pl.BlockSpec((d, f), lambda i: (0, 0)),
            pl.BlockSpec((d, f), lambda i: (0, 0)),
            pl.BlockSpec((f, d), lambda i: (0, 0)),
            pl.BlockSpec((1, d), lambda i: (0, 0)),
        ],
        out_specs=pl.BlockSpec((tm, d), lambda i: (i, 0)),
        out_shape=jax.ShapeDtypeStruct((t, d), F32),
        scratch_shapes=[pltpu.VMEM((2, TOP_K, tm, d), F32),
                        pltpu.SemaphoreType.DMA((2,))],
        compiler_params=_params("arbitrary"),
        name="combine",
    )(dest, dest, ys, h2, x1_flat, gt2, ws_g, ws_u, ws_d, g_final.reshape(1, d))


def kernel(x, c, positions, g_norm1, w_ada, b_ada, w_in, w_dw, b_dw, g_conv_ln, b_conv_ln, w_pw2, b_pw2, g_attn_out, g_conv_out, w_out, g_norm2, w_router, b_router, w_exp_gate, w_exp_up, w_exp_down, w_sh_gate, w_sh_up, w_sh_down, g_final):
    bsz, seq, d = x.shape
    depth = w_ada.shape[0]
    assert depth == 1, "the final RMSNorm is fused into the single layer's combine kernel"
    attn_w = g_attn_out.shape[-1]
    conv_w = g_conv_out.shape[-1]
    cos_l, sin_l = _rope_tables(positions)
    for l in range(depth):
        ada = _ada(c, w_ada[l], b_ada[l])
        sh1, sc1, gt1, sh2, sc2, gt2 = [t.reshape(bsz, 1, d) for t in jnp.split(ada, 6, axis=-1)]
        q4, k4, v4, a = _inproj(x, sc1, sh1, g_norm1[l].reshape(1, d), w_in[l].astype(BF16),
                                cos_l, sin_l, attn_w, conv_w)
        branches = [_attn_branch(q4, k4, v4, w, dl) for (w, dl) in DILATED_BRANCHES]
        x1, h2 = _mix(branches, a, x, gt1, sc2, sh2, w_dw[l], b_dw[l], g_conv_ln[l], b_conv_ln[l],
                      w_pw2[l].astype(BF16), b_pw2[l], g_attn_out[l], g_conv_out[l],
                      w_out[l].astype(BF16), g_norm2[l])
        idx, wts = _router(h2, w_router[l].T, b_router[l])
        block_e, n_used, rows_tok, rows_w, dest, nblk = _dispatch(idx, wts)
        ys = _experts(h2, block_e, n_used, rows_tok, rows_w, nblk,
                      w_exp_gate[l], w_exp_up[l], w_exp_down[l])
        xf = x1.reshape(bsz * seq, d)
        out = _combine(ys, dest, h2, xf, gt2, w_sh_gate[l].astype(BF16), w_sh_up[l].astype(BF16),
                       w_sh_down[l].astype(BF16), g_final, seq)
        x = out.reshape(bsz, seq, d)
    return x
```

```python
import functools

import jax
import jax.numpy as jnp
from jax import lax
from jax.experimental import pallas as pl
from jax.experimental.pallas import tpu as pltpu

F32 = jnp.float32
BF16 = jnp.bfloat16
I32 = jnp.int32

HEAD_DIM = 64
LANES = 128
ROPE_DIM = HEAD_DIM // 4
ROPE_THETA = 500000.0
DILATED_BRANCHES = ((128, 1), (512, 4), (2048, 16))
ATTN_BLOCK = 128
ATTN_CHUNK = 2048
CONV_KERNEL = 31
CONV_HALO = 32
N_EXPERTS = 256
TOP_K = 8
N_GROUPS = 8
TOPK_GROUPS = 4
ROUTED_SCALE = 2.5
NORM_EPS = 1e-6
NEG_INF = -1e30
NEG_BIG = -3e38

TM_INPROJ = 512
TM_MIX = 256
TM_ROUTER = 256
MOE_ROWS = 256
TM_COMBINE = 256
VMEM_LIMIT = 56 * 1024 * 1024


def _params(*sem):
    return pltpu.CompilerParams(dimension_semantics=sem, vmem_limit_bytes=VMEM_LIMIT)


def _rms(x, g):
    return x * lax.rsqrt(jnp.mean(x * x, axis=-1, keepdims=True) + NORM_EPS) * g


def _tile_rows_load(ref, rows):
    nchunk = ref.shape[0] // rows
    return jnp.concatenate([ref[pl.ds(c, rows, stride=nchunk), :] for c in range(nchunk)], axis=1)


def _tile_rows_store(ref, val):
    nchunk = val.shape[1] // LANES
    for c in range(nchunk):
        ref[pl.ds(c, val.shape[0], stride=nchunk), :] = val[:, c * LANES:(c + 1) * LANES]


def _ada_kernel(c_ref, w_ref, b_ref, o_ref):
    s = c_ref[...]
    s = s * jax.nn.sigmoid(s)
    o_ref[...] = jnp.dot(s, w_ref[...], precision=lax.Precision.HIGHEST,
                         preferred_element_type=F32) + b_ref[...]


def _ada(c, w_ada, b_ada):
    bsz, d = c.shape
    n = w_ada.shape[1]
    rows = 8
    cp = jnp.pad(c, ((0, rows - bsz), (0, 0)))
    tn = 1024
    out = pl.pallas_call(
        _ada_kernel,
        grid=(n // tn,),
        in_specs=[pl.BlockSpec((rows, d), lambda j: (0, 0)),
                  pl.BlockSpec((d, tn), lambda j: (0, j)),
                  pl.BlockSpec((1, tn), lambda j: (0, j))],
        out_specs=pl.BlockSpec((rows, tn), lambda j: (0, j)),
        out_shape=jax.ShapeDtypeStruct((rows, n), F32),
        compiler_params=_params("arbitrary"),
        name="ada",
    )(cp, w_ada, b_ada.reshape(1, n))
    return out[:bsz]


def _rope_kernel(pos_ref, f_ref, cos_ref, sin_ref):
    ang = pos_ref[...].astype(F32) * f_ref[...]
    cos_ref[...] = jnp.cos(ang)
    sin_ref[...] = jnp.sin(ang)


def _rope_tables(positions):
    half = ROPE_DIM // 2
    t = positions.size
    per_row = LANES // half
    inv_freq = ROPE_THETA ** (-(jnp.arange(half, dtype=F32) * 2.0) / ROPE_DIM)
    pos_rep = jnp.repeat(positions.reshape(t // per_row, per_row), half, axis=1)
    f_lane = jnp.tile(inv_freq, per_row).reshape(1, LANES)
    cos_t, sin_t = pl.pallas_call(
        _rope_kernel,
        out_shape=(jax.ShapeDtypeStruct((t // per_row, LANES), F32),) * 2,
        name="rope",
    )(pos_rep, f_lane)
    cos8 = cos_t.reshape(t, half)
    sin8 = sin_t.reshape(t, half)
    rest = HEAD_DIM - ROPE_DIM
    cos_h = jnp.concatenate([cos8, cos8, jnp.ones((t, rest), F32)], axis=1)
    sin_h = jnp.concatenate([sin8, sin8, jnp.zeros((t, rest), F32)], axis=1)
    return jnp.tile(cos_h, (1, LANES // HEAD_DIM)), jnp.tile(sin_h, (1, LANES // HEAD_DIM))


def _inproj_kernel(x_ref, sc_ref, sh_ref, g_ref, w_ref, cos_ref, sin_ref,
                   q_ref, k_ref, v_ref, a_ref, *, attn_w, conv_w):
    x = x_ref[0]
    h = _rms(x, g_ref[...]) * (1.0 + sc_ref[0]) + sh_ref[0]
    proj = jnp.dot(h.astype(BF16), w_ref[...], preferred_element_type=F32)
    cosv = cos_ref[...]
    sinv = sin_ref[...]
    lane = lax.broadcasted_iota(I32, cosv.shape, 1)
    first_half = (lane & (HEAD_DIM - 1)) < (ROPE_DIM // 2)

    def rope(t):
        up = pltpu.roll(t, LANES - ROPE_DIM // 2, 1)
        dn = pltpu.roll(t, ROPE_DIM // 2, 1)
        return t * cosv + jnp.where(first_half, -up, dn) * sinv

    scale = HEAD_DIM ** -0.5
    for hp in range(attn_w // LANES):
        lo = hp * LANES
        q_ref[0, hp] = (rope(proj[:, lo:lo + LANES]) * scale).astype(BF16)
        k_ref[0, hp] = rope(proj[:, attn_w + lo:attn_w + lo + LANES]).astype(BF16)
        v_ref[0, hp] = proj[:, 2 * attn_w + lo:2 * attn_w + lo + LANES].astype(BF16)
    cu = proj[:, 3 * attn_w:3 * attn_w + conv_w]
    cg = proj[:, 3 * attn_w + conv_w:]
    a_ref[0] = cu * jax.nn.sigmoid(cg)


def _inproj(x, sc1, sh1, g1, w_in_bf16, cos_l, sin_l, attn_w, conv_w):
    bsz, seq, d = x.shape
    tm = min(TM_INPROJ, seq)
    n = w_in_bf16.shape[1]
    nhp = attn_w // LANES
    nt = seq // tm
    qkv_shape = jax.ShapeDtypeStruct((bsz, nhp, seq, LANES), BF16)
    qkv_spec = pl.BlockSpec((1, nhp, tm, LANES), lambda b, i: (b, 0, i, 0))
    vec_spec = pl.BlockSpec((1, 1, d), lambda b, i: (b, 0, 0))
    return pl.pallas_call(
        functools.partial(_inproj_kernel, attn_w=attn_w, conv_w=conv_w),
        grid=(bsz, nt),
        in_specs=[pl.BlockSpec((1, tm, d), lambda b, i: (b, i, 0)),
                  vec_spec, vec_spec,
                  pl.BlockSpec((1, d), lambda b, i: (0, 0)),
                  pl.BlockSpec((d, n), lambda b, i: (0, 0)),
                  pl.BlockSpec((tm, LANES), lambda b, i: (b * nt + i, 0)),
                  pl.BlockSpec((tm, LANES), lambda b, i: (b * nt + i, 0))],
        out_specs=[qkv_spec, qkv_spec, qkv_spec,
                   pl.BlockSpec((1, tm, conv_w), lambda b, i: (b, i, 0))],
        out_shape=[qkv_shape, qkv_shape, qkv_shape,
                   jax.ShapeDtypeStruct((bsz, seq, conv_w), F32)],
        compiler_params=_params("arbitrary", "arbitrary"),
        name="inproj",
    )(x, sc1, sh1, g1, w_in_bf16, cos_l, sin_l)


def _attn_kernel(q_ref, k_ref, v_ref, kp_ref, vp_ref, o_ref, l_ref, bias_ref, *, dil, n_back):
    blk = ATTN_BLOCK
    nblk = q_ref.shape[2] // blk
    qi = lax.broadcasted_iota(I32, (2 * blk, 2 * blk), 0) & (blk - 1)
    kj = lax.broadcasted_iota(I32, (2 * blk, 2 * blk), 1)
    dist = blk + qi - kj
    band = (dist >= 0) & (dist <= n_back)
    kmin = jnp.where(pl.program_id(2) > 0, 0, blk)
    bias_ref[0] = jnp.where(band & (kj >= kmin), 0.0, NEG_INF)
    bias_ref[1] = jnp.where(band, 0.0, NEG_INF)
    lane = lax.broadcasted_iota(I32, (blk, LANES), 1)
    lo = lane < HEAD_DIM

    for j in range(nblk):
        rows = slice(j * blk, (j + 1) * blk)
        prows = slice((j - 1) * blk, j * blk)
        for r in range(dil):
            cols = slice(r * LANES, (r + 1) * LANES)
            qb = q_ref[0, 0, rows, cols]
            if j == 0:
                kprev = kp_ref[0, 0, :, cols]
                vprev = vp_ref[0, 0, :, cols]
            else:
                kprev = k_ref[0, 0, prows, cols]
                vprev = v_ref[0, 0, prows, cols]
            kk = jnp.concatenate([kprev, k_ref[0, 0, rows, cols]], axis=0)
            vv = jnp.concatenate([vprev, v_ref[0, 0, rows, cols]], axis=0)
            zero = jnp.zeros_like(qb)
            q2 = jnp.concatenate([jnp.where(lo, qb, zero), jnp.where(lo, zero, qb)], axis=0)
            s = lax.dot_general(q2, kk, (((1,), (1,)), ((), ())), preferred_element_type=F32)
            s = s + bias_ref[0 if j == 0 else 1]
            m = jnp.max(s, axis=-1, keepdims=True)
            p = jnp.exp(s - m)
            l = jnp.sum(p, axis=-1, keepdims=True)
            pv = jnp.dot(p.astype(BF16), vv, preferred_element_type=F32)
            o2 = pv * (1.0 / l)
            lse2 = m + jnp.log(l)
            o_ref[0, 0, rows, cols] = jnp.where(lo, o2[:blk], o2[blk:])
            l_ref[0, 0, rows, cols] = jnp.where(lo, jnp.broadcast_to(lse2[:blk], (blk, LANES)),
                                                jnp.broadcast_to(lse2[blk:], (blk, LANES)))


def _attn_branch(q4, k4, v4, window, dil):
    bsz, nhp, seq, _ = q4.shape
    n_back = window // dil
    assert n_back <= ATTN_BLOCK and ATTN_CHUNK % (dil * ATTN_BLOCK) == 0 and seq % ATTN_CHUNK == 0
    rows = ATTN_CHUNK // dil
    nblk = rows // ATTN_BLOCK
    width = dil * LANES
    view = lambda t: t.reshape(bsz, nhp, seq // dil, width)
    cur = pl.BlockSpec((1, 1, rows, width), lambda b, h, n: (b, h, n, 0))
    prev = pl.BlockSpec((1, 1, ATTN_BLOCK, width),
                        lambda b, h, n: (b, h, jnp.maximum(n * nblk - 1, 0), 0))
    out_shape = jax.ShapeDtypeStruct((bsz, nhp, seq // dil, width), F32)
    o, l = pl.pallas_call(
        functools.partial(_attn_kernel, dil=dil, n_back=n_back),
        grid=(bsz, nhp, seq // ATTN_CHUNK),
        in_specs=[cur, cur, cur, prev, prev],
        out_specs=[cur, cur],
        out_shape=[out_shape, out_shape],
        scratch_shapes=[pltpu.VMEM((2, 2 * ATTN_BLOCK, 2 * ATTN_BLOCK), F32)],
        compiler_params=_params("arbitrary", "arbitrary", "arbitrary"),
        name=f"attn_d{dil}",
    )(view(q4), view(k4), view(v4), view(k4), view(v4))
    return o.reshape(bsz, nhp, seq, LANES), l.reshape(bsz, nhp, seq, LANES)


def _mix_kernel(o1_ref, l1_ref, o4_ref, l4_ref, o16_ref, l16_ref, a_ref, ap_ref, x_ref,
                gt_ref, sc_ref, sh_ref, wdw_ref, bdw_ref, lng_ref, lnb_ref, wpw_ref, bpw_ref,
                ga_ref, gc_ref, wout_ref, g2_ref, x1_ref, h2_ref, acat_ref):
    tm = a_ref.shape[1]
    acat_ref[CONV_HALO:, :] = a_ref[0]
    halo = ap_ref[0]
    acat_ref[:CONV_HALO, :] = jnp.where(pl.program_id(1) > 0, halo, jnp.zeros_like(halo))
    off = CONV_HALO - (CONV_KERNEL - 1)
    y = acat_ref[pl.ds(off, tm), :] * wdw_ref[0:1, :]
    for j in range(1, CONV_KERNEL):
        y = y + acat_ref[pl.ds(off + j, tm), :] * wdw_ref[j:j + 1, :]
    y = y + bdw_ref[...]
    mu = jnp.mean(y, axis=-1, keepdims=True)
    yc = y - mu
    var = jnp.mean(yc * yc, axis=-1, keepdims=True)
    yn = yc * lax.rsqrt(var + NORM_EPS) * lng_ref[...] + lnb_ref[...]
    z = yn * jax.nn.sigmoid(yn)
    conv = jnp.dot(z.astype(BF16), wpw_ref[...], preferred_element_type=F32) + bpw_ref[...]
    nc = _rms(conv, gc_ref[...])
    parts = []
    for hp in range(o1_ref.shape[1]):
        l1, l4, l16 = l1_ref[0, hp], l4_ref[0, hp], l16_ref[0, hp]
        m = jnp.maximum(jnp.maximum(l1, l4), l16)
        e1, e4, e16 = jnp.exp(l1 - m), jnp.exp(l4 - m), jnp.exp(l16 - m)
        num = e1 * o1_ref[0, hp] + e4 * o4_ref[0, hp] + e16 * o16_ref[0, hp]
        parts.append(num * (1.0 / (e1 + e4 + e16)))
    attn = jnp.concatenate(parts, axis=1)
    na = _rms(attn, ga_ref[...])
    cat = jnp.concatenate([na, nc], axis=1).astype(BF16)
    mixed = jnp.dot(cat, wout_ref[...], preferred_element_type=F32)
    x1 = x_ref[0] + gt_ref[0] * mixed
    x1_ref[0] = x1
    _tile_rows_store(h2_ref, _rms(x1, g2_ref[...]) * (1.0 + sc_ref[0]) + sh_ref[0])


def _mix(branches, a, x, gt1, sc2, sh2, w_dw, b_dw, ln_g, ln_b, w_pw2_bf16, b_pw2,
         g_attn, g_conv, w_out_bf16, g2):
    bsz, seq, d = x.shape
    cw = a.shape[-1]
    tm = min(TM_MIX, seq)
    nt = seq // tm
    nhp = branches[0][0].shape[1]
    br_spec = pl.BlockSpec((1, nhp, tm, LANES), lambda b, i: (b, 0, i, 0))
    vec_d = pl.BlockSpec((1, 1, d), lambda b, i: (b, 0, 0))
    full = lambda arr: pl.BlockSpec(arr.shape, lambda b, i: (0,) * arr.ndim)
    row = lambda v: v.reshape(1, -1)
    halo_blocks = tm // CONV_HALO
    consts = [w_dw, row(b_dw), row(ln_g), row(ln_b), w_pw2_bf16, row(b_pw2),
              row(g_attn), row(g_conv), w_out_bf16, row(g2)]
    flat = [t for pair in branches for t in pair]
    x1, h2 = pl.pallas_call(
        _mix_kernel,
        grid=(bsz, nt),
        in_specs=[br_spec] * 6 + [
            pl.BlockSpec((1, tm, cw), lambda b, i: (b, i, 0)),
            pl.BlockSpec((1, CONV_HALO, cw),
                         lambda b, i: (b, jnp.maximum(i * halo_blocks - 1, 0), 0)),
            pl.BlockSpec((1, tm, d), lambda b, i: (b, i, 0)),
            vec_d, vec_d, vec_d] + [full(t) for t in consts],
        out_specs=[pl.BlockSpec((1, tm, d), lambda b, i: (b, i, 0)),
                   pl.BlockSpec((tm * d // LANES, LANES), lambda b, i: (b * nt + i, 0))],
        out_shape=[jax.ShapeDtypeStruct((bsz, seq, d), F32),
                   jax.ShapeDtypeStruct((bsz * seq * d // LANES, LANES), F32)],
        scratch_shapes=[pltpu.VMEM((tm + CONV_HALO, cw), F32)],
        compiler_params=_params("arbitrary", "arbitrary"),
        name="mix",
    )(*flat, a, a, x, gt1, sc2, sh2, *consts)
    return x1, h2


def _router_kernel(h_ref, w_ref, b_ref, idx_ref, wts_ref, rank_ref, cnt_ref):
    tm = idx_ref.shape[1]
    h = _tile_rows_load(h_ref, tm)
    w = w_ref[...]
    h_hi = h.astype(BF16)
    h_lo = (h - h_hi.astype(F32)).astype(BF16)
    w_hi = w.astype(BF16)
    w_lo = (w - w_hi.astype(F32)).astype(BF16)
    nt = (((1,), (1,)), ((), ()))
    logits = (lax.dot_general(w_hi, h_hi, nt, preferred_element_type=F32)
              + lax.dot_general(w_hi, h_lo, nt, preferred_element_type=F32)
              + lax.dot_general(w_lo, h_hi, nt, preferred_element_type=F32))
    scores = jax.nn.sigmoid(logits)
    biased = scores + b_ref[...]
    ne, tm = biased.shape
    gsz = ne // N_GROUPS
    gio = lax.broadcasted_iota(I32, (gsz, tm), 0).astype(F32)
    gscore = []
    for g in range(N_GROUPS):
        blk = biased[g * gsz:(g + 1) * gsz]
        m1 = jnp.max(blk, axis=0, keepdims=True)
        i1 = jnp.min(jnp.where(blk == m1, gio, float(gsz)), axis=0, keepdims=True)
        m2 = jnp.max(jnp.where(gio == i1, NEG_BIG, blk), axis=0, keepdims=True)
        gscore.append(m1 + m2)
    masked = []
    for g in range(N_GROUPS):
        beaten = jnp.zeros_like(gscore[g])
        for o in range(N_GROUPS):
            if o == g:
                continue
            wins = (gscore[o] >= gscore[g]) if o < g else (gscore[o] > gscore[g])
            beaten = beaten + jnp.where(wins, 1.0, 0.0)
        keep = beaten < float(TOPK_GROUPS)
        masked.append(jnp.where(keep, biased[g * gsz:(g + 1) * gsz], NEG_INF))
    cur = jnp.concatenate(masked, axis=0)
    eio = lax.broadcasted_iota(I32, (ne, tm), 0).astype(F32)
    picks, vals = [], []
    for _ in range(TOP_K):
        m = jnp.max(cur, axis=0, keepdims=True)
        ix = jnp.min(jnp.where(cur == m, eio, float(ne)), axis=0, keepdims=True)
        hit = eio == ix
        vals.append(jnp.sum(jnp.where(hit, scores, 0.0), axis=0, keepdims=True))
        cur = jnp.where(hit, NEG_BIG, cur)
        picks.append(ix)
    total = vals[0]
    for v in vals[1:]:
        total = total + v
    for k in range(TOP_K):
        idx_ref[k:k + 1, :] = picks[k].astype(I32)
        wts_ref[k:k + 1, :] = vals[k] / total * ROUTED_SCALE

    @pl.when(pl.program_id(0) == 0)
    def _():
        cnt_ref[...] = jnp.zeros_like(cnt_ref)

    member = jnp.where(eio == picks[0], 1.0, 0.0)
    for k in range(1, TOP_K):
        member = member + jnp.where(eio == picks[k], 1.0, 0.0)
    member = member.astype(BF16)
    earlier = (lax.broadcasted_iota(I32, (tm, tm), 0) < lax.broadcasted_iota(I32, (tm, tm), 1))
    prefix = jnp.dot(member, jnp.where(earlier, 1.0, 0.0).astype(BF16), preferred_element_type=F32)
    base = cnt_ref[...]
    before = jnp.concatenate([base] * (tm // LANES), axis=1) + prefix
    for k in range(TOP_K):
        rank_ref[k:k + 1, :] = jnp.sum(jnp.where(eio == picks[k], before, 0.0), axis=0,
                                       keepdims=True).astype(I32)
    cnt_ref[...] = base + jnp.dot(member, jnp.ones((tm, LANES), BF16), preferred_element_type=F32)


def _router(h2_tiles, w_router_t, b_router):
    ne, d = w_router_t.shape
    nchunk = d // LANES
    t = h2_tiles.shape[0] // nchunk
    tm = min(TM_ROUTER, t)
    kt_spec = pl.BlockSpec((TOP_K, tm), lambda i: (0, i))
    return pl.pallas_call(
        _router_kernel,
        grid=(t // tm,),
        in_specs=[pl.BlockSpec((tm * nchunk, LANES), lambda i: (i, 0)),
                  pl.BlockSpec((ne, d), lambda i: (0, 0)),
                  pl.BlockSpec((ne, 1), lambda i: (0, 0))],
        out_specs=[kt_spec, kt_spec, kt_spec, pl.BlockSpec((ne, LANES), lambda i: (0, 0))],
        out_shape=[jax.ShapeDtypeStruct((TOP_K, t), I32),
                   jax.ShapeDtypeStruct((TOP_K, t), F32),
                   jax.ShapeDtypeStruct((TOP_K, t), I32),
                   jax.ShapeDtypeStruct((ne, LANES), F32)],
        compiler_params=_params("arbitrary"),
        name="router",
    )(h2_tiles, w_router_t, b_router.reshape(ne, 1))


def _block_plan(counts, n_assign):
    blk = MOE_ROWS
    ne = counts.shape[0]
    padded = ((counts + blk - 1) // blk) * blk
    pend = jnp.cumsum(padded)
    pstart = pend - padded
    nblk = -(-n_assign // blk) + ne
    n_used = pend[-1] // blk
    first_row = jnp.arange(nblk, dtype=I32) * blk
    block_e = jnp.minimum(jnp.sum(pend[None, :] <= first_row[:, None], axis=1), ne - 1).astype(I32)
    onehot = (block_e[:, None] == jnp.arange(ne, dtype=I32)[None, :]).astype(I32)
    pick = lambda v: jnp.sum(onehot * v[None, :], axis=1)
    n_valid = jnp.clip(pick(counts) - (first_row - pick(pstart)), 0, blk).astype(I32)
    eids = jnp.arange(ne, dtype=I32)
    later = (eids[None, :] > eids[:, None]) & (padded[None, :] > 0)
    nxt = jnp.min(jnp.where(later, eids[None, :], ne), axis=1)
    nxt = jnp.where(nxt == ne, -1, nxt).astype(I32)
    return pstart.astype(I32), block_e, n_used.astype(I32).reshape(1), n_valid, pick(nxt), nblk


def _dest_kernel(idx_ref, rank_ref, ps_ref, dest_ref):
    ne = ps_ref.shape[0]
    tm = idx_ref.shape[1]
    eio = lax.broadcasted_iota(I32, (ne, tm), 0)
    ps = ps_ref[...]
    for k in range(TOP_K):
        hit = eio == idx_ref[k:k + 1, :]
        first = jnp.sum(jnp.where(hit, ps, 0.0), axis=0, keepdims=True)
        dest_ref[k:k + 1, :] = first.astype(I32) + rank_ref[k:k + 1, :]


def _dest_rows(idx, rank, pstart):
    k, t = idx.shape
    ne = pstart.shape[0]
    tm = min(1024, t)
    spec = pl.BlockSpec((k, tm), lambda i: (0, i))
    return pl.pallas_call(
        _dest_kernel,
        grid=(t // tm,),
        in_specs=[spec, spec, pl.BlockSpec((ne, 1), lambda i: (0, 0))],
        out_specs=spec,
        out_shape=jax.ShapeDtypeStruct((k, t), I32),
        compiler_params=_params("arbitrary"),
        name="dest",
    )(idx, rank, pstart.astype(F32).reshape(ne, 1))


def _tile_copy(src, src_tok, dst, dst_tok, sem, rows):
    return pltpu.make_async_copy(src.at[pl.ds(src_tok * rows, rows), :],
                                 dst.at[pl.ds(dst_tok * rows, rows), :], sem)


def _scatter_kernel(dest_ref, h_hbm, xs_hbm, hbuf, lsem, ssem):
    i = pl.program_id(0)
    n = pl.num_programs(0)
    slot = i % 2
    tm = dest_ref.shape[1]
    rows = hbuf.shape[1] // tm

    def load(step, s):
        return pltpu.make_async_copy(h_hbm.at[pl.ds(step * tm * rows, tm * rows), :], hbuf.at[s],
                                     lsem.at[s])

    def scatter(s, wait):
        def body(t, carry):
            for k in range(TOP_K):
                if wait:
                    _tile_copy(hbuf.at[s], t, xs_hbm, 0, ssem.at[s], rows).wait()
                else:
                    _tile_copy(hbuf.at[s], t, xs_hbm, dest_ref[k, t], ssem.at[s], rows).start()
            return carry
        lax.fori_loop(0, tm, body, 0, unroll=4)

    @pl.when(i == 0)
    def _():
        load(i, slot).start()

    @pl.when(i > 0)
    def _():
        scatter(1 - slot, True)

    @pl.when(i + 1 < n)
    def _():
        load(i + 1, 1 - slot).start()

    load(i, slot).wait()
    scatter(slot, False)

    @pl.when(i == n - 1)
    def _():
        scatter(slot, True)


def _scatter_rows(h2_tiles, dest, n_rows, nchunk):
    k, t = dest.shape
    tm = min(TM_COMBINE, t)
    return pl.pallas_call(
        _scatter_kernel,
        grid=(t // tm,),
        in_specs=[pl.BlockSpec((k, tm), lambda i: (0, i), memory_space=pltpu.SMEM),
                  pl.BlockSpec(memory_space=pl.ANY)],
        out_specs=pl.BlockSpec(memory_space=pl.ANY),
        out_shape=jax.ShapeDtypeStruct((n_rows * nchunk, LANES), F32),
        scratch_shapes=[pltpu.VMEM((2, tm * nchunk, LANES), F32),
                        pltpu.SemaphoreType.DMA((2,)),
                        pltpu.SemaphoreType.DMA((2,))],
        compiler_params=_params("arbitrary"),
        name="scatter",
    )(dest, h2_tiles)


def _experts_kernel(be_ref, nu_ref, nv_ref, nx_ref, xs_ref, wg_hbm, wu_hbm, wd_hbm, ys_ref,
                    wgf, wuf, wdf, wsem, wgb, wub, wdb, wslot):
    i = pl.program_id(0)
    nu = nu_ref[0]
    e = be_ref[i]
    rows = MOE_ROWS

    def weight_copies(expert, s):
        return (pltpu.make_async_copy(wg_hbm.at[expert], wgf.at[s], wsem.at[s, 0]),
                pltpu.make_async_copy(wu_hbm.at[expert], wuf.at[s], wsem.at[s, 1]),
                pltpu.make_async_copy(wd_hbm.at[expert], wdf.at[s], wsem.at[s, 2]))

    @pl.when(i == 0)
    def _():
        wslot[0] = 0
        for cp in weight_copies(e, 0):
            cp.start()

    fresh = jnp.logical_or(i == 0, e != be_ref[jnp.maximum(i - 1, 0)])

    @pl.when(jnp.logical_and(fresh, i < nu))
    def _():
        s = wslot[0]
        for cp in weight_copies(e, s):
            cp.wait()
        wgb[...] = wgf[s].astype(BF16)
        wub[...] = wuf[s].astype(BF16)
        wdb[...] = wdf[s].astype(BF16)
        nxt = nx_ref[i]

        @pl.when(nxt >= 0)
        def _():
            for cp in weight_copies(nxt, 1 - s):
                cp.start()
        wslot[0] = 1 - s

    @pl.when(i < nu)
    def _():
        x = _tile_rows_load(xs_ref, rows)
        live = lax.broadcasted_iota(I32, (rows, 1), 0) < nv_ref[i]
        xb = jnp.where(live, x, 0.0).astype(BF16)
        g = jnp.dot(xb, wgb[...], preferred_element_type=F32)
        u = jnp.dot(xb, wub[...], preferred_element_type=F32)
        act = (g * jax.nn.sigmoid(g)) * u
        _tile_rows_store(ys_ref, jnp.dot(act.astype(BF16), wdb[...], preferred_element_type=F32))

    @pl.when(i >= nu)
    def _():
        ys_ref[...] = jnp.zeros_like(ys_ref)


def _experts(xs_tiles, block_e, n_used, n_valid, next_e, nblk, w_g, w_u, w_d):
    _, d, f = w_g.shape
    nchunk = d // LANES
    blk = MOE_ROWS
    any_spec = pl.BlockSpec(memory_space=pl.ANY)
    grid_spec = pltpu.PrefetchScalarGridSpec(
        num_scalar_prefetch=4,
        grid=(nblk,),
        in_specs=[pl.BlockSpec((blk * nchunk, LANES),
                               lambda i, be, nu, nv, nx: (jnp.minimum(i, nu[0] - 1), 0)),
                  any_spec, any_spec, any_spec],
        out_specs=pl.BlockSpec((blk * nchunk, LANES), lambda i, be, nu, nv, nx: (i, 0)),
        scratch_shapes=[pltpu.VMEM((2, d, f), F32), pltpu.VMEM((2, d, f), F32),
                        pltpu.VMEM((2, f, d), F32),
                        pltpu.SemaphoreType.DMA((2, 3)),
                        pltpu.VMEM((d, f), BF16), pltpu.VMEM((d, f), BF16),
                        pltpu.VMEM((f, d), BF16),
                        pltpu.SMEM((1,), I32)],
    )
    return pl.pallas_call(
        _experts_kernel,
        grid_spec=grid_spec,
        out_shape=jax.ShapeDtypeStruct((nblk * blk * nchunk, LANES), F32),
        compiler_params=_params("arbitrary"),
        name="experts",
    )(block_e, n_used, n_valid, next_e, xs_tiles, w_g, w_u, w_d)


def _combine_kernel(dc_ref, dn_ref, ys_hbm, w_ref, h_ref, x_ref, gt_ref, wsg_ref, wsu_ref, wsd_ref,
                    gf_ref, o_ref, buf, sem):
    i = pl.program_id(0)
    n = pl.num_programs(0)
    slot = i % 2
    tm = x_ref.shape[0]
    rows = h_ref.shape[0] // tm

    def gather(d_ref, s, wait):
        def body(t, carry):
            for k in range(TOP_K):
                if wait:
                    _tile_copy(ys_hbm, 0, buf.at[s, k], t, sem.at[s], rows).wait()
                else:
                    _tile_copy(ys_hbm, d_ref[k, t], buf.at[s, k], t, sem.at[s], rows).start()
            return carry
        lax.fori_loop(0, tm, body, 0, unroll=4)

    @pl.when(i == 0)
    def _():
        gather(dc_ref, slot, False)

    @pl.when(i + 1 < n)
    def _():
        gather(dn_ref, 1 - slot, False)

    h = _tile_rows_load(h_ref, tm)
    hb = h.astype(BF16)
    g = jnp.dot(hb, wsg_ref[...], preferred_element_type=F32)
    u = jnp.dot(hb, wsu_ref[...], preferred_element_type=F32)
    act = (g * jax.nn.sigmoid(g)) * u
    y = jnp.dot(act.astype(BF16), wsd_ref[...], preferred_element_type=F32)
    gather(dc_ref, slot, True)
    w = w_ref[...]
    routed = w[:, 0:1] * _tile_rows_load(buf.at[slot, 0], tm)
    for k in range(1, TOP_K):
        routed = routed + w[:, k:k + 1] * _tile_rows_load(buf.at[slot, k], tm)
    xo = x_ref[...] + gt_ref[0] * (routed + y)
    o_ref[...] = _rms(xo, gf_ref[...])


def _combine(ys_tiles, dest, wts_t, h2_tiles, x1_flat, gt2, ws_g, ws_u, ws_d, g_final, seq):
    t, d = x1_flat.shape
    f = ws_g.shape[1]
    nchunk = d // LANES
    tm = min(TM_COMBINE, seq)
    nt = t // tm
    per_seq = seq // tm
    return pl.pallas_call(
        _combine_kernel,
        grid=(nt,),
        in_specs=[
            pl.BlockSpec((TOP_K, tm), lambda i: (0, i), memory_space=pltpu.SMEM),
            pl.BlockSpec((TOP_K, tm), lambda i: (0, jnp.minimum(i + 1, nt - 1)),
                         memory_space=pltpu.SMEM),
            pl.BlockSpec(memory_space=pl.ANY),
            pl.BlockSpec((tm, TOP_K), lambda i: (i, 0)),
            pl.BlockSpec((tm * nchunk, LANES), lambda i: (i, 0)),
            pl.BlockSpec((tm, d), lambda i: (i, 0)),
            pl.BlockSpec((1, 1, d), lambda i: (i // per_seq, 0, 0)),
            pl.BlockSpec((d, f), lambda i: (0, 0)),
            pl.BlockSpec((d, f), lambda i: (0, 0)),
            pl.BlockSpec((f, d), lambda i: (0, 0)),
            pl.BlockSpec((1, d), lambda i: (0, 0)),
        ],
        out_specs=pl.BlockSpec((tm, d), lambda i: (i, 0)),
        out_shape=jax.ShapeDtypeStruct((t, d), F32),
        scratch_shapes=[pltpu.VMEM((2, TOP_K, tm * nchunk, LANES), F32),
                        pltpu.SemaphoreType.DMA((2,))],
        compiler_params=_params("arbitrary"),
        name="combine",
    )(dest, dest, ys_tiles, wts_t, h2_tiles, x1_flat, gt2, ws_g, ws_u, ws_d, g_final.reshape(1, d))


def kernel(x, c, positions, g_norm1, w_ada, b_ada, w_in, w_dw, b_dw, g_conv_ln, b_conv_ln, w_pw2, b_pw2, g_attn_out, g_conv_out, w_out, g_norm2, w_router, b_router, w_exp_gate, w_exp_up, w_exp_down, w_sh_gate, w_sh_up, w_sh_down, g_final):
    bsz, seq, d = x.shape
    depth = w_ada.shape[0]
    assert depth == 1, "the final RMSNorm is fused into the single layer's combine kernel"
    attn_w = g_attn_out.shape[-1]
    conv_w = g_conv_out.shape[-1]
    cos_l, sin_l = _rope_tables(positions)
    for l in range(depth):
        ada = _ada(c, w_ada[l], b_ada[l])
        sh1, sc1, gt1, sh2, sc2, gt2 = [t.reshape(bsz, 1, d) for t in jnp.split(ada, 6, axis=-1)]
        q4, k4, v4, a = _inproj(x, sc1, sh1, g_norm1[l].reshape(1, d), w_in[l].astype(BF16),
                                cos_l, sin_l, attn_w, conv_w)
        branches = [_attn_branch(q4, k4, v4, w, dl) for (w, dl) in DILATED_BRANCHES]
        x1, h2 = _mix(branches, a, x, gt1, sc2, sh2, w_dw[l], b_dw[l], g_conv_ln[l], b_conv_ln[l],
                      w_pw2[l].astype(BF16), b_pw2[l], g_attn_out[l], g_conv_out[l],
                      w_out[l].astype(BF16), g_norm2[l])
        idx, wts, rank, cnt = _router(h2, w_router[l].T, b_router[l])
        counts = cnt[:, 0].astype(I32)
        pstart, block_e, n_used, n_valid, next_e, nblk = _block_plan(counts, idx.size)
        dest = _dest_rows(idx, rank, pstart)
        xs = _scatter_rows(h2, dest, nblk * MOE_ROWS, d // LANES)
        ys = _experts(xs, block_e, n_used, n_valid, next_e, nblk,
                      w_exp_gate[l], w_exp_up[l], w_exp_down[l])
        xf = x1.reshape(bsz * seq, d)
        out = _combine(ys, dest, wts.T, h2, xf, gt2, w_sh_gate[l].astype(BF16),
                       w_sh_up[l].astype(BF16), w_sh_down[l].astype(BF16), g_final, seq)
        x = out.reshape(bsz, seq, d)
    return x
```

```python
import functools

import jax
import jax.numpy as jnp
from jax import lax
from jax.experimental import pallas as pl
from jax.experimental.pallas import tpu as pltpu

F32 = jnp.float32
BF16 = jnp.bfloat16
I32 = jnp.int32

HEAD_DIM = 64
LANES = 128
SUBLANES = 8
ROPE_DIM = HEAD_DIM // 4
ROPE_THETA = 500000.0
DILATED_BRANCHES = ((128, 1), (512, 4), (2048, 16))
ATTN_BLOCK = 128
ATTN_CHUNK = 2048
CONV_KERNEL = 31
CONV_HALO = 32
N_EXPERTS = 256
TOP_K = 8
N_GROUPS = 8
TOPK_GROUPS = 4
ROUTED_SCALE = 2.5
NORM_EPS = 1e-6
NEG_INF = -1e30
NEG_BIG = -3e38

TM_INPROJ = 512
TM_MIX = 512
TM_ROUTER = 256
MOE_ROWS = 256
MOE_RING = 4
TAG_BITS = 18
TAG_TILE = 1024
TAG_WINDOW = 2 * TAG_TILE
GATHER_AHEAD = 2
TAG_RING = 8
TM_COMBINE = 256
VMEM_LIMIT = 56 * 1024 * 1024


def _params(*sem):
    return pltpu.CompilerParams(dimension_semantics=sem, vmem_limit_bytes=VMEM_LIMIT)


def _rms(x, g):
    return x * lax.rsqrt(jnp.mean(x * x, axis=-1, keepdims=True) + NORM_EPS) * g


def _rows_store(ref, val):
    rows, c = val.shape
    nchunk = c // LANES
    for j in range(nchunk):
        ref[pl.ds(j, rows, stride=nchunk), :] = val[:, j * LANES:(j + 1) * LANES]


def _rows_load(ref, rows):
    nchunk = ref.shape[0] // rows
    return jnp.concatenate([ref[pl.ds(j, rows, stride=nchunk), :] for j in range(nchunk)], axis=1)


def _ada_kernel(c_ref, w_ref, b_ref, o_ref):
    s = c_ref[...]
    s = s * jax.nn.sigmoid(s)
    o_ref[...] = jnp.dot(s, w_ref[...], precision=lax.Precision.HIGHEST,
                         preferred_element_type=F32) + b_ref[...]


def _ada(c, w_ada, b_ada):
    bsz, d = c.shape
    n = w_ada.shape[1]
    rows = 8
    cp = jnp.pad(c, ((0, rows - bsz), (0, 0)))
    tn = 1024
    out = pl.pallas_call(
        _ada_kernel,
        grid=(n // tn,),
        in_specs=[pl.BlockSpec((rows, d), lambda j: (0, 0)),
                  pl.BlockSpec((d, tn), lambda j: (0, j)),
                  pl.BlockSpec((1, tn), lambda j: (0, j))],
        out_specs=pl.BlockSpec((rows, tn), lambda j: (0, j)),
        out_shape=jax.ShapeDtypeStruct((rows, n), F32),
        compiler_params=_params("arbitrary"),
        name="ada",
    )(cp, w_ada, b_ada.reshape(1, n))
    return out[:bsz]


def _rope_kernel(pos_ref, f_ref, cos_ref, sin_ref):
    ang = pos_ref[...].astype(F32) * f_ref[...]
    cos_ref[...] = jnp.cos(ang)
    sin_ref[...] = jnp.sin(ang)


def _rope_tables(positions):
    half = ROPE_DIM // 2
    t = positions.size
    per_row = LANES // half
    inv_freq = ROPE_THETA ** (-(jnp.arange(half, dtype=F32) * 2.0) / ROPE_DIM)
    pos_rep = jnp.repeat(positions.reshape(t // per_row, per_row), half, axis=1)
    f_lane = jnp.tile(inv_freq, per_row).reshape(1, LANES)
    cos_t, sin_t = pl.pallas_call(
        _rope_kernel,
        out_shape=(jax.ShapeDtypeStruct((t // per_row, LANES), F32),) * 2,
        name="rope",
    )(pos_rep, f_lane)
    cos8 = cos_t.reshape(t, half)
    sin8 = sin_t.reshape(t, half)
    rest = HEAD_DIM - ROPE_DIM
    cos_h = jnp.concatenate([cos8, cos8, jnp.ones((t, rest), F32)], axis=1)
    sin_h = jnp.concatenate([sin8, sin8, jnp.zeros((t, rest), F32)], axis=1)
    return jnp.tile(cos_h, (1, LANES // HEAD_DIM)), jnp.tile(sin_h, (1, LANES // HEAD_DIM))


def _inproj_kernel(x_ref, sc_ref, sh_ref, g_ref, w_ref, cos_ref, sin_ref,
                   q_ref, k_ref, v_ref, a_ref, *, attn_w, conv_w):
    x = x_ref[0]
    h = _rms(x, g_ref[...]) * (1.0 + sc_ref[0]) + sh_ref[0]
    proj = jnp.dot(h.astype(BF16), w_ref[...], preferred_element_type=F32)
    cosv = cos_ref[...]
    sinv = sin_ref[...]
    lane = lax.broadcasted_iota(I32, cosv.shape, 1)
    first_half = (lane & (HEAD_DIM - 1)) < (ROPE_DIM // 2)

    def rope(t):
        up = pltpu.roll(t, LANES - ROPE_DIM // 2, 1)
        dn = pltpu.roll(t, ROPE_DIM // 2, 1)
        return t * cosv + jnp.where(first_half, -up, dn) * sinv

    scale = HEAD_DIM ** -0.5
    for hp in range(attn_w // LANES):
        lo = hp * LANES
        q_ref[0, hp] = rope(proj[:, lo:lo + LANES]) * scale
        k_ref[0, hp] = rope(proj[:, attn_w + lo:attn_w + lo + LANES])
        v_ref[0, hp] = proj[:, 2 * attn_w + lo:2 * attn_w + lo + LANES]
    cu = proj[:, 3 * attn_w:3 * attn_w + conv_w]
    cg = proj[:, 3 * attn_w + conv_w:]
    a_ref[0] = cu * jax.nn.sigmoid(cg)


def _inproj(x, sc1, sh1, g1, w_in_bf16, cos_l, sin_l, attn_w, conv_w):
    bsz, seq, d = x.shape
    tm = min(TM_INPROJ, seq)
    n = w_in_bf16.shape[1]
    nhp = attn_w // LANES
    nt = seq // tm
    qkv_shape = jax.ShapeDtypeStruct((bsz, nhp, seq, LANES), F32)
    qkv_spec = pl.BlockSpec((1, nhp, tm, LANES), lambda b, i: (b, 0, i, 0))
    vec_spec = pl.BlockSpec((1, 1, d), lambda b, i: (b, 0, 0))
    return pl.pallas_call(
        functools.partial(_inproj_kernel, attn_w=attn_w, conv_w=conv_w),
        grid=(bsz, nt),
        in_specs=[pl.BlockSpec((1, tm, d), lambda b, i: (b, i, 0)),
                  vec_spec, vec_spec,
                  pl.BlockSpec((1, d), lambda b, i: (0, 0)),
                  pl.BlockSpec((d, n), lambda b, i: (0, 0)),
                  pl.BlockSpec((tm, LANES), lambda b, i: (b * nt + i, 0)),
                  pl.BlockSpec((tm, LANES), lambda b, i: (b * nt + i, 0))],
        out_specs=[qkv_spec, qkv_spec, qkv_spec,
                   pl.BlockSpec((1, tm, conv_w), lambda b, i: (b, i, 0))],
        out_shape=[qkv_shape, qkv_shape, qkv_shape,
                   jax.ShapeDtypeStruct((bsz, seq, conv_w), F32)],
        compiler_params=_params("arbitrary", "arbitrary"),
        name="inproj",
    )(x, sc1, sh1, g1, w_in_bf16, cos_l, sin_l)


def _attn_kernel(q_ref, k_ref, v_ref, kp_ref, vp_ref, o_ref, bias_ref, ob_ref, lb_ref, *, n_back):
    blk = ATTN_BLOCK
    chunk = q_ref.shape[2]
    qi = lax.broadcasted_iota(I32, (2 * blk, 2 * blk), 0) & (blk - 1)
    kj = lax.broadcasted_iota(I32, (2 * blk, 2 * blk), 1)
    dist = blk + qi - kj
    band = (dist >= 0) & (dist <= n_back)
    kmin = jnp.where(pl.program_id(2) > 0, 0, blk)
    bias_ref[0] = jnp.where(band & (kj >= kmin), 0.0, NEG_INF)
    bias_ref[1] = jnp.where(band, 0.0, NEG_INF)
    lane = lax.broadcasted_iota(I32, (blk, LANES), 1)
    lo = lane < HEAD_DIM

    for bi, (_, dil) in enumerate(DILATED_BRANCHES):
        span = blk * dil

        def rows(start, dil=dil):
            return pl.ds(start, blk, stride=dil) if dil > 1 else pl.ds(start, blk)

        for j in range(chunk // span):
            for r in range(dil):
                base = j * span + r
                qb = q_ref[0, 0, rows(base), :].astype(BF16)
                if j == 0:
                    kprev = kp_ref[0, 0, rows(chunk - span + r), :]
                    vprev = vp_ref[0, 0, rows(chunk - span + r), :]
                else:
                    kprev = k_ref[0, 0, rows(base - span), :]
                    vprev = v_ref[0, 0, rows(base - span), :]
                kk = jnp.concatenate([kprev, k_ref[0, 0, rows(base), :]], axis=0).astype(BF16)
                vv = jnp.concatenate([vprev, v_ref[0, 0, rows(base), :]], axis=0).astype(BF16)
                zero = jnp.zeros_like(qb)
                q2 = jnp.concatenate([jnp.where(lo, qb, zero), jnp.where(lo, zero, qb)], axis=0)
                s = lax.dot_general(q2, kk, (((1,), (1,)), ((), ())), preferred_element_type=F32)
                s = s + bias_ref[0 if j == 0 else 1]
                m = jnp.max(s, axis=-1, keepdims=True)
                p = jnp.exp(s - m)
                l = jnp.sum(p, axis=-1, keepdims=True)
                pv = jnp.dot(p.astype(BF16), vv, preferred_element_type=F32)
                o2 = pv * (1.0 / l)
                lse2 = m + jnp.log(l)
                ob_ref[bi, rows(base), :] = jnp.where(lo, o2[:blk], o2[blk:])
                lb_ref[bi, rows(base), :] = jnp.where(
                    lo, jnp.broadcast_to(lse2[:blk], (blk, LANES)),
                    jnp.broadcast_to(lse2[blk:], (blk, LANES)))

    step = 2 * blk
    for c in range(chunk // step):
        sl = slice(c * step, (c + 1) * step)
        ls = [lb_ref[bi, sl, :] for bi in range(len(DILATED_BRANCHES))]
        m = functools.reduce(jnp.maximum, ls)
        es = [jnp.exp(l - m) for l in ls]
        num = functools.reduce(lambda a, b: a + b,
                               [e * ob_ref[bi, sl, :] for bi, e in enumerate(es)])
        o_ref[0, 0, sl, :] = num * (1.0 / functools.reduce(lambda a, b: a + b, es))


def _attention(q4, k4, v4):
    bsz, nhp, seq, _ = q4.shape
    n_back = DILATED_BRANCHES[0][0] // DILATED_BRANCHES[0][1]
    assert all(w // dl == n_back for w, dl in DILATED_BRANCHES) and n_back <= ATTN_BLOCK
    assert all(ATTN_CHUNK % (dl * ATTN_BLOCK) == 0 for _, dl in DILATED_BRANCHES)
    assert seq % ATTN_CHUNK == 0
    cur = pl.BlockSpec((1, 1, ATTN_CHUNK, LANES), lambda b, h, n: (b, h, n, 0))
    prev = pl.BlockSpec((1, 1, ATTN_CHUNK, LANES), lambda b, h, n: (b, h, jnp.maximum(n - 1, 0), 0))
    nbr = len(DILATED_BRANCHES)
    return pl.pallas_call(
        functools.partial(_attn_kernel, n_back=n_back),
        grid=(bsz, nhp, seq // ATTN_CHUNK),
        in_specs=[cur, cur, cur, prev, prev],
        out_specs=cur,
        out_shape=jax.ShapeDtypeStruct((bsz, nhp, seq, LANES), F32),
        scratch_shapes=[pltpu.VMEM((2, 2 * ATTN_BLOCK, 2 * ATTN_BLOCK), F32),
                        pltpu.VMEM((nbr, ATTN_CHUNK, LANES), F32),
                        pltpu.VMEM((nbr, ATTN_CHUNK, LANES), F32)],
        compiler_params=_params("arbitrary", "arbitrary", "arbitrary"),
        name="attn",
    )(q4, k4, v4, k4, v4)


def _mix_kernel(at_ref, a_ref, ap_ref, x_ref,
                gt_ref, sc_ref, sh_ref, wdw_ref, bdw_ref, lng_ref, lnb_ref, wpw_ref, bpw_ref,
                ga_ref, gc_ref, wout_ref, g2_ref, x1_ref, h2_ref, h2p_ref, acat_ref, shift_ref):
    tm = a_ref.shape[1]
    acat_ref[CONV_HALO:, :] = a_ref[0]
    halo = ap_ref[0]
    acat_ref[:CONV_HALO, :] = jnp.where(pl.program_id(1) > 0, halo, jnp.zeros_like(halo))
    sub = shift_ref.shape[0]
    keep = shift_ref.shape[1]
    for s in range(1, sub):
        shift_ref[s] = acat_ref[pl.ds(s, keep), :]
    off = CONV_HALO - (CONV_KERNEL - 1)

    def tap(j):
        a, s = divmod(off + j, sub)
        src = acat_ref if s == 0 else shift_ref.at[s]
        return src[pl.ds(a * sub, tm), :] * wdw_ref[j:j + 1, :]

    y = tap(0)
    for j in range(1, CONV_KERNEL):
        y = y + tap(j)
    y = y + bdw_ref[...]
    mu = jnp.mean(y, axis=-1, keepdims=True)
    yc = y - mu
    var = jnp.mean(yc * yc, axis=-1, keepdims=True)
    yn = yc * lax.rsqrt(var + NORM_EPS) * lng_ref[...] + lnb_ref[...]
    z = yn * jax.nn.sigmoid(yn)
    conv = jnp.dot(z.astype(BF16), wpw_ref[...], preferred_element_type=F32) + bpw_ref[...]
    nc = _rms(conv, gc_ref[...])
    attn = jnp.concatenate([at_ref[0, hp] for hp in range(at_ref.shape[1])], axis=1)
    na = _rms(attn, ga_ref[...])
    cat = jnp.concatenate([na, nc], axis=1).astype(BF16)
    mixed = jnp.dot(cat, wout_ref[...], preferred_element_type=F32)
    x1 = x_ref[0] + gt_ref[0] * mixed
    x1_ref[0] = x1
    h2 = _rms(x1, g2_ref[...]) * (1.0 + sc_ref[0]) + sh_ref[0]
    h2_ref[...] = h2
    _rows_store(h2p_ref, h2)


def _mix(attn4, a, x, gt1, sc2, sh2, w_dw, b_dw, ln_g, ln_b, w_pw2_bf16, b_pw2,
         g_attn, g_conv, w_out_bf16, g2):
    bsz, seq, d = x.shape
    cw = a.shape[-1]
    tm = min(TM_MIX, seq)
    nt = seq // tm
    nhp = attn4.shape[1]
    br_spec = pl.BlockSpec((1, nhp, tm, LANES), lambda b, i: (b, 0, i, 0))
    vec_d = pl.BlockSpec((1, 1, d), lambda b, i: (b, 0, 0))
    full = lambda arr: pl.BlockSpec(arr.shape, lambda b, i: (0,) * arr.ndim)
    row = lambda v: v.reshape(1, -1)
    halo_blocks = tm // CONV_HALO
    consts = [w_dw, row(b_dw), row(ln_g), row(ln_b), w_pw2_bf16, row(b_pw2),
              row(g_attn), row(g_conv), w_out_bf16, row(g2)]
    prow = d // LANES
    return pl.pallas_call(
        _mix_kernel,
        grid=(bsz, nt),
        in_specs=[br_spec] + [
            pl.BlockSpec((1, tm, cw), lambda b, i: (b, i, 0)),
            pl.BlockSpec((1, CONV_HALO, cw),
                         lambda b, i: (b, jnp.maximum(i * halo_blocks - 1, 0), 0)),
            pl.BlockSpec((1, tm, d), lambda b, i: (b, i, 0)),
            vec_d, vec_d, vec_d] + [full(t) for t in consts],
        out_specs=[pl.BlockSpec((1, tm, d), lambda b, i: (b, i, 0)),
                   pl.BlockSpec((tm, d), lambda b, i: (b * nt + i, 0)),
                   pl.BlockSpec((tm * prow, LANES), lambda b, i: (b * nt + i, 0))],
        out_shape=[jax.ShapeDtypeStruct((bsz, seq, d), F32),
                   jax.ShapeDtypeStruct((bsz * seq, d), F32),
                   jax.ShapeDtypeStruct((bsz * seq * prow, LANES), F32)],
        scratch_shapes=[pltpu.VMEM((tm + CONV_HALO, cw), F32),
                        pltpu.VMEM((SUBLANES, tm + CONV_HALO - SUBLANES, cw), F32)],
        compiler_params=_params("arbitrary", "arbitrary"),
        name="mix",
    )(attn4, a, a, x, gt1, sc2, sh2, *consts)


def _router_kernel(h_ref, w_ref, b_ref, idx_ref, wts_ref, cnt_ref, whi_ref, wlo_ref):
    tm = idx_ref.shape[1]

    @pl.when(pl.program_id(0) == 0)
    def _():
        w = w_ref[...]
        whi_ref[...] = w.astype(BF16)
        wlo_ref[...] = (w - w.astype(BF16).astype(F32)).astype(BF16)
        cnt_ref[...] = jnp.zeros_like(cnt_ref)

    h = h_ref[...]
    h_hi = h.astype(BF16)
    h_lo = (h - h_hi.astype(F32)).astype(BF16)
    w_hi = whi_ref[...]
    w_lo = wlo_ref[...]
    nt = (((1,), (1,)), ((), ()))
    logits = (lax.dot_general(w_hi, h_hi, nt, preferred_element_type=F32)
              + lax.dot_general(w_hi, h_lo, nt, preferred_element_type=F32)
              + lax.dot_general(w_lo, h_hi, nt, preferred_element_type=F32))
    scores = jax.nn.sigmoid(logits)
    biased = scores + b_ref[...]
    ne, tm = biased.shape
    gsz = ne // N_GROUPS
    gio = lax.broadcasted_iota(I32, (gsz, tm), 0).astype(F32)
    gscore = []
    for g in range(N_GROUPS):
        blk = biased[g * gsz:(g + 1) * gsz]
        m1 = jnp.max(blk, axis=0, keepdims=True)
        i1 = jnp.min(jnp.where(blk == m1, gio, float(gsz)), axis=0, keepdims=True)
        m2 = jnp.max(jnp.where(gio == i1, NEG_BIG, blk), axis=0, keepdims=True)
        gscore.append(m1 + m2)
    masked = []
    for g in range(N_GROUPS):
        beaten = jnp.zeros_like(gscore[g])
        for o in range(N_GROUPS):
            if o == g:
                continue
            wins = (gscore[o] >= gscore[g]) if o < g else (gscore[o] > gscore[g])
            beaten = beaten + jnp.where(wins, 1.0, 0.0)
        keep = beaten < float(TOPK_GROUPS)
        masked.append(jnp.where(keep, biased[g * gsz:(g + 1) * gsz], NEG_INF))
    cur = jnp.concatenate(masked, axis=0)
    eio = lax.broadcasted_iota(I32, (ne, tm), 0).astype(F32)
    picks, vals = [], []
    member = jnp.zeros_like(scores)
    for _ in range(TOP_K):
        m = jnp.max(cur, axis=0, keepdims=True)
        ix = jnp.min(jnp.where(cur == m, eio, float(ne)), axis=0, keepdims=True)
        hit = eio == ix
        vals.append(jnp.sum(jnp.where(hit, scores, 0.0), axis=0, keepdims=True))
        cur = jnp.where(hit, NEG_BIG, cur)
        member = jnp.where(hit, 1.0, member)
        picks.append(ix)
    total = vals[0]
    for v in vals[1:]:
        total = total + v
    for k in range(TOP_K):
        idx_ref[k:k + 1, :] = picks[k].astype(I32)
        wts_ref[k:k + 1, :] = vals[k] / total * ROUTED_SCALE

    cnt_ref[...] += jnp.dot(member.astype(BF16), jnp.ones((tm, LANES), BF16),
                            preferred_element_type=F32)


def _router(h2, w_router_t, b_router):
    ne, d = w_router_t.shape
    t = h2.shape[0]
    tm = min(TM_ROUTER, t)
    kt_spec = pl.BlockSpec((TOP_K, tm), lambda i: (0, i))
    return pl.pallas_call(
        _router_kernel,
        grid=(t // tm,),
        in_specs=[pl.BlockSpec((tm, d), lambda i: (i, 0)),
                  pl.BlockSpec((ne, d), lambda i: (0, 0)),
                  pl.BlockSpec((ne, 1), lambda i: (0, 0))],
        out_specs=[kt_spec, kt_spec, pl.BlockSpec((ne, LANES), lambda i: (0, 0))],
        out_shape=[jax.ShapeDtypeStruct((TOP_K, t), I32),
                   jax.ShapeDtypeStruct((TOP_K, t), F32),
                   jax.ShapeDtypeStruct((ne, LANES), F32)],
        scratch_shapes=[pltpu.VMEM((ne, d), BF16), pltpu.VMEM((ne, d), BF16)],
        compiler_params=_params("arbitrary"),
        name="router",
    )(h2, w_router_t, b_router.reshape(ne, 1))


def _block_plan(counts, n_assign):
    blk = MOE_ROWS
    ne = counts.shape[0]
    padded = ((counts + blk - 1) // blk) * blk
    pend = jnp.cumsum(padded)
    pstart = pend - padded
    nblk = -(-n_assign // blk) + ne
    cstart = jnp.cumsum(counts) - counts
    first_row = jnp.arange(nblk, dtype=I32) * blk
    block_e = jnp.minimum(jnp.sum(pend[None, :] <= first_row[:, None], axis=1), ne - 1)
    onehot = (block_e[:, None] == jnp.arange(ne, dtype=I32)[None, :]).astype(I32)
    pick = lambda v: jnp.sum(onehot * v[None, :], axis=1)
    within = first_row - pick(pstart)
    blk_first = jnp.minimum(pick(cstart) + within, n_assign).astype(I32)
    blk_first = jnp.pad(blk_first, (0, GATHER_AHEAD + 2), constant_values=n_assign)
    blk_valid = jnp.clip(pick(counts) - within, 0, blk).astype(I32)
    return (pstart // blk).astype(I32), (pend[-1:] // blk).astype(I32), blk_first, blk_valid


def _sorted_tags(idx):
    k, t = idx.shape
    a = k * t
    assert a <= (1 << TAG_BITS) and (N_EXPERTS << TAG_BITS) <= (1 << 31)
    keys = (idx.T.reshape(-1) << TAG_BITS) | jnp.arange(a, dtype=I32)
    order = jnp.sort(keys) & ((1 << TAG_BITS) - 1)
    token = order // k
    slot = (order % k) * t + token
    return jnp.pad(slot, (0, TAG_WINDOW)), jnp.pad(token, (0, TAG_WINDOW))


def _tile_copy(src, src_tok, dst, dst_tok, sem, rows):
    return pltpu.make_async_copy(src.at[pl.ds(src_tok * rows, rows), :],
                                 dst.at[pl.ds(dst_tok * rows, rows), :], sem)


def _experts_kernel(fb_ref, cnt_ref, nu_ref, bf_ref, bv_ref, h_hbm, stag_hbm, ttag_hbm, wg_hbm,
                    wu_hbm, wd_hbm, yk_hbm, xbuf, ybuf, xsem, ysem, stagbuf, ttagbuf, tsem,
                    wgf, wuf, wdf, wsem, wgb, wub, wdb):
    e = pl.program_id(0)
    ne = pl.num_programs(0)
    nu = nu_ref[0]
    rows = MOE_ROWS
    ring, brows = xbuf.shape[0], xbuf.shape[1]
    prow = brows // rows
    cnt = cnt_ref[e]
    first = fb_ref[e]
    nb = lax.shift_right_logical(cnt + (rows - 1), rows.bit_length() - 1)
    dump = yk_hbm.shape[0] // prow - ring * rows

    def tag_copies(g):
        ts = g % TAG_RING
        start = pl.multiple_of(bf_ref[g] & ~(TAG_TILE - 1), TAG_TILE)
        window = pl.ds(ts * TAG_WINDOW, TAG_WINDOW)
        return (pltpu.make_async_copy(stag_hbm.at[pl.ds(start, TAG_WINDOW)], stagbuf.at[window],
                                      tsem.at[ts, 0]),
                pltpu.make_async_copy(ttag_hbm.at[pl.ds(start, TAG_WINDOW)], ttagbuf.at[window],
                                      tsem.at[ts, 1]))

    def tag_base(g):
        g = jnp.maximum(g, 0)
        return (g % TAG_RING) * TAG_WINDOW + (bf_ref[g] & (TAG_TILE - 1))

    def x_rows_wait(s):
        pltpu.make_async_copy(h_hbm.at[pl.ds(0, brows), :], xbuf.at[s], xsem.at[s]).wait()

    def y_rows_wait(s):
        pltpu.make_async_copy(ybuf.at[s], yk_hbm.at[pl.ds(0, brows), :], ysem.at[s]).wait()

    def gather_rows(g, s):
        base = tag_base(g)
        for r in range(rows):
            _tile_copy(h_hbm, ttagbuf[base + r], xbuf.at[s], r, xsem.at[s],
                       prow).start(priority=r % 2)

    def send_rows(g, valid, s):
        base = tag_base(g)
        for r in range(rows):
            dst = jnp.where(r < valid, stagbuf[base + r], dump + s * rows + r)
            _tile_copy(ybuf.at[s], r, yk_hbm, dst, ysem.at[s], prow).start(priority=r % 2)

    def weight_copies(expert, s):
        return (pltpu.make_async_copy(wg_hbm.at[expert], wgf.at[s], wsem.at[s, 0]),
                pltpu.make_async_copy(wu_hbm.at[expert], wuf.at[s], wsem.at[s, 1]),
                pltpu.make_async_copy(wd_hbm.at[expert], wdf.at[s], wsem.at[s, 2]))

    @pl.when(e == 0)
    def _():
        for cp in weight_copies(0, 0):
            cp.start(priority=1)
        for g in range(GATHER_AHEAD + 1):
            for cp in tag_copies(g):
                cp.start()
        for g in range(GATHER_AHEAD):
            for cp in tag_copies(g):
                cp.wait()
            gather_rows(g, g)

    @pl.when(e + 1 < ne)
    def _():
        for cp in weight_copies(e + 1, (e + 1) % 2):
            cp.start(priority=1)

    for cp in weight_copies(e, e % 2):
        cp.wait()

    @pl.when(nb > 0)
    def _():
        wgb[...] = wgf[e % 2].astype(BF16)
        wub[...] = wuf[e % 2].astype(BF16)
        wdb[...] = wdf[e % 2].astype(BF16)

    def block(j, carry):
        g = first + j
        s = g % ring
        x_rows_wait(s)
        for cp in tag_copies(g + GATHER_AHEAD):
            cp.wait()
        for cp in tag_copies(g + GATHER_AHEAD + 1):
            cp.start()

        @pl.when(g >= ring - 1)
        def _():
            y_rows_wait(s)

        prev_valid = jnp.where(g > 0, bv_ref[jnp.maximum(g - 1, 0)], 0)

        for slot in range(ring):
            @pl.when(s == slot)
            def _(slot=slot):
                x = _rows_load(xbuf.at[slot], rows)
                live = lax.broadcasted_iota(I32, (rows, 1), 0) < cnt - j * rows
                xb = jnp.where(live, x, 0.0).astype(BF16)
                gate = jnp.dot(xb, wgb[...], preferred_element_type=F32)
                up = jnp.dot(xb, wub[...], preferred_element_type=F32)
                act = (gate * jax.nn.sigmoid(gate)) * up
                y = jnp.dot(act.astype(BF16), wdb[...], preferred_element_type=F32)
                send_rows(g - 1, prev_valid, (slot - 1) % ring)
                gather_rows(g + GATHER_AHEAD, (slot + GATHER_AHEAD) % ring)
                _rows_store(ybuf.at[slot], y)
        return carry

    lax.fori_loop(0, nb, block, 0)

    @pl.when(e == ne - 1)
    def _():
        send_rows(nu - 1, bv_ref[nu - 1], (nu - 1) % ring)
        for g in range(GATHER_AHEAD):
            x_rows_wait((nu + g) % ring)
        for cp in tag_copies(nu + GATHER_AHEAD):
            cp.wait()
        for back in range(ring, 0, -1):
            @pl.when(nu - back >= -1)
            def _(back=back):
                y_rows_wait((nu - back) % ring)


def _experts(h2_packed, slot_tags, token_tags, first_blk, counts, n_used, blk_first, blk_valid,
             n_slots, w_g, w_u, w_d):
    ne, d, f = w_g.shape
    prow = d // LANES
    brows = MOE_ROWS * prow
    assert GATHER_AHEAD + 2 <= MOE_RING and GATHER_AHEAD + 3 <= TAG_RING
    any_spec = pl.BlockSpec(memory_space=pl.ANY)
    grid_spec = pltpu.PrefetchScalarGridSpec(
        num_scalar_prefetch=5,
        grid=(ne,),
        in_specs=[any_spec] * 6,
        out_specs=any_spec,
        scratch_shapes=[pltpu.VMEM((MOE_RING, brows, LANES), F32),
                        pltpu.VMEM((MOE_RING, brows, LANES), F32),
                        pltpu.SemaphoreType.DMA((MOE_RING,)), pltpu.SemaphoreType.DMA((MOE_RING,)),
                        pltpu.SMEM((TAG_RING * TAG_WINDOW,), I32),
                        pltpu.SMEM((TAG_RING * TAG_WINDOW,), I32),
                        pltpu.SemaphoreType.DMA((TAG_RING, 2)),
                        pltpu.VMEM((2, d, f), F32), pltpu.VMEM((2, d, f), F32),
                        pltpu.VMEM((2, f, d), F32), pltpu.SemaphoreType.DMA((2, 3)),
                        pltpu.VMEM((d, f), BF16), pltpu.VMEM((d, f), BF16),
                        pltpu.VMEM((f, d), BF16)],
    )
    return pl.pallas_call(
        _experts_kernel,
        grid_spec=grid_spec,
        out_shape=jax.ShapeDtypeStruct(((n_slots + MOE_RING * MOE_ROWS) * prow, LANES), F32),
        compiler_params=_params("arbitrary"),
        name="experts",
    )(first_blk, counts, n_used, blk_first, blk_valid, h2_packed, slot_tags, token_tags,
      w_g, w_u, w_d)


def _combine_kernel(*refs):
    yk_refs = refs[:TOP_K]
    w_ref, h_ref, x_ref, gt_ref, wsg_ref, wsu_ref, wsd_ref, gf_ref, o_ref = refs[TOP_K:]
    tm = x_ref.shape[0]
    hb = _rows_load(h_ref, tm).astype(BF16)
    g = jnp.dot(hb, wsg_ref[...], preferred_element_type=F32)
    u = jnp.dot(hb, wsu_ref[...], preferred_element_type=F32)
    act = (g * jax.nn.sigmoid(g)) * u
    y = jnp.dot(act.astype(BF16), wsd_ref[...], preferred_element_type=F32)
    w = w_ref[...]
    routed = w[:, 0:1] * _rows_load(yk_refs[0], tm)
    for k in range(1, TOP_K):
        routed = routed + w[:, k:k + 1] * _rows_load(yk_refs[k], tm)
    xo = x_ref[...] + gt_ref[0] * (routed + y)
    o_ref[...] = _rms(xo, gf_ref[...])


def _combine(yk, wts_t, h2_packed, x1_flat, gt2, ws_g, ws_u, ws_d, g_final, seq):
    t, d = x1_flat.shape
    f = ws_g.shape[1]
    nchunk = d // LANES
    tm = min(TM_COMBINE, seq)
    nt = t // tm
    per_seq = seq // tm
    slot_spec = lambda k: pl.BlockSpec((tm * nchunk, LANES), lambda i: (k * nt + i, 0))
    return pl.pallas_call(
        _combine_kernel,
        grid=(nt,),
        in_specs=[slot_spec(k) for k in range(TOP_K)] + [
            pl.BlockSpec((tm, TOP_K), lambda i: (i, 0)),
            pl.BlockSpec((tm * nchunk, LANES), lambda i: (i, 0)),
            pl.BlockSpec((tm, d), lambda i: (i, 0)),
            pl.BlockSpec((1, 1, d), lambda i: (i // per_seq, 0, 0)),
            pl.BlockSpec((d, f), lambda i: (0, 0)),
            pl.BlockSpec((d, f), lambda i: (0, 0)),
            pl.BlockSpec((f, d), lambda i: (0, 0)),
            pl.BlockSpec((1, d), lambda i: (0, 0)),
        ],
        out_specs=pl.BlockSpec((tm, d), lambda i: (i, 0)),
        out_shape=jax.ShapeDtypeStruct((t, d), F32),
        compiler_params=_params("arbitrary"),
        name="combine",
    )(*([yk] * TOP_K), wts_t, h2_packed, x1_flat, gt2, ws_g, ws_u, ws_d, g_final.reshape(1, d))


def kernel(x, c, positions, g_norm1, w_ada, b_ada, w_in, w_dw, b_dw, g_conv_ln, b_conv_ln, w_pw2, b_pw2, g_attn_out, g_conv_out, w_out, g_norm2, w_router, b_router, w_exp_gate, w_exp_up, w_exp_down, w_sh_gate, w_sh_up, w_sh_down, g_final):
    bsz, seq, d = x.shape
    depth = w_ada.shape[0]
    assert depth == 1, "the final RMSNorm is fused into the single layer's combine kernel"
    attn_w = g_attn_out.shape[-1]
    conv_w = g_conv_out.shape[-1]
    cos_l, sin_l = _rope_tables(positions)
    for l in range(depth):
        ada = _ada(c, w_ada[l], b_ada[l])
        sh1, sc1, gt1, sh2, sc2, gt2 = [t.reshape(bsz, 1, d) for t in jnp.split(ada, 6, axis=-1)]
        q4, k4, v4, a = _inproj(x, sc1, sh1, g_norm1[l].reshape(1, d), w_in[l].astype(BF16),
                                cos_l, sin_l, attn_w, conv_w)
        attn4 = _attention(q4, k4, v4)
        x1, h2, h2p = _mix(attn4, a, x, gt1, sc2, sh2, w_dw[l], b_dw[l], g_conv_ln[l], b_conv_ln[l],
                      w_pw2[l].astype(BF16), b_pw2[l], g_attn_out[l], g_conv_out[l],
                      w_out[l].astype(BF16), g_norm2[l])
        idx, wts, cnt = _router(h2, w_router[l].T, b_router[l])
        counts = cnt[:, 0].astype(I32)
        first_blk, n_used, blk_first, blk_valid = _block_plan(counts, idx.size)
        slot_tags, token_tags = _sorted_tags(idx)
        yk = _experts(h2p, slot_tags, token_tags, first_blk, counts, n_used, blk_first, blk_valid,
                      idx.size, w_exp_gate[l], w_exp_up[l], w_exp_down[l])
        xf = x1.reshape(bsz * seq, d)
        out = _combine(yk, wts.T, h2p, xf, gt2, w_sh_gate[l].astype(BF16),
                       w_sh_up[l].astype(BF16), w_sh_down[l].astype(BF16), g_final, seq)
        x = out.reshape(bsz, seq, d)
    return x
```

```python
import functools

import jax
import jax.numpy as jnp
from jax import lax
from jax.experimental import pallas as pl
from jax.experimental.pallas import tpu as pltpu

F32 = jnp.float32
BF16 = jnp.bfloat16
I32 = jnp.int32

HEAD_DIM = 64
LANES = 128
SUBLANES = 8
ROPE_DIM = HEAD_DIM // 4
ROPE_THETA = 500000.0
DILATED_BRANCHES = ((128, 1), (512, 4), (2048, 16))
ATTN_BLOCK = 128
ATTN_CHUNK = 2048
CONV_KERNEL = 31
CONV_HALO = 32
N_EXPERTS = 256
TOP_K = 8
N_GROUPS = 8
TOPK_GROUPS = 4
ROUTED_SCALE = 2.5
NORM_EPS = 1e-6
NEG_INF = -1e30
NEG_BIG = -3e38

TM_INPROJ = 512
TM_MIX = 512
TM_ROUTER = 256
MOE_ROWS = 256
MOE_RING = 4
TAG_BITS = 18
TAG_TILE = 1024
TAG_WINDOW = 2 * TAG_TILE
GATHER_AHEAD = 2
TAG_RING = 8
TM_COMBINE = 512
VMEM_LIMIT = 56 * 1024 * 1024


def _params(*sem):
    return pltpu.CompilerParams(dimension_semantics=sem, vmem_limit_bytes=VMEM_LIMIT)


def _rms(x, g):
    return x * lax.rsqrt(jnp.mean(x * x, axis=-1, keepdims=True) + NORM_EPS) * g


def _rows_store(ref, val):
    rows, c = val.shape
    nchunk = c // LANES
    for j in range(nchunk):
        ref[pl.ds(j, rows, stride=nchunk), :] = val[:, j * LANES:(j + 1) * LANES]


def _rows_load(ref, rows):
    nchunk = ref.shape[0] // rows
    return jnp.concatenate([ref[pl.ds(j, rows, stride=nchunk), :] for j in range(nchunk)], axis=1)


def _ada_kernel(c_ref, w_ref, b_ref, o_ref):
    s = c_ref[...]
    s = s * jax.nn.sigmoid(s)
    o_ref[...] = jnp.dot(s, w_ref[...], precision=lax.Precision.HIGHEST,
                         preferred_element_type=F32) + b_ref[...]


def _ada(c, w_ada, b_ada):
    bsz, d = c.shape
    n = w_ada.shape[1]
    rows = 8
    cp = jnp.pad(c, ((0, rows - bsz), (0, 0)))
    tn = 1024
    out = pl.pallas_call(
        _ada_kernel,
        grid=(n // tn,),
        in_specs=[pl.BlockSpec((rows, d), lambda j: (0, 0)),
                  pl.BlockSpec((d, tn), lambda j: (0, j)),
                  pl.BlockSpec((1, tn), lambda j: (0, j))],
        out_specs=pl.BlockSpec((rows, tn), lambda j: (0, j)),
        out_shape=jax.ShapeDtypeStruct((rows, n), F32),
        compiler_params=_params("arbitrary"),
        name="ada",
    )(cp, w_ada, b_ada.reshape(1, n))
    return out[:bsz]


def _rope_kernel(pos_ref, f_ref, cos_ref, sin_ref):
    ang = pos_ref[...].astype(F32) * f_ref[...]
    cos_ref[...] = jnp.cos(ang)
    sin_ref[...] = jnp.sin(ang)


def _rope_tables(positions):
    half = ROPE_DIM // 2
    t = positions.size
    per_row = LANES // half
    inv_freq = ROPE_THETA ** (-(jnp.arange(half, dtype=F32) * 2.0) / ROPE_DIM)
    pos_rep = jnp.repeat(positions.reshape(t // per_row, per_row), half, axis=1)
    f_lane = jnp.tile(inv_freq, per_row).reshape(1, LANES)
    cos_t, sin_t = pl.pallas_call(
        _rope_kernel,
        out_shape=(jax.ShapeDtypeStruct((t // per_row, LANES), F32),) * 2,
        name="rope",
    )(pos_rep, f_lane)
    cos8 = cos_t.reshape(t, half)
    sin8 = sin_t.reshape(t, half)
    rest = HEAD_DIM - ROPE_DIM
    cos_h = jnp.concatenate([cos8, cos8, jnp.ones((t, rest), F32)], axis=1)
    sin_h = jnp.concatenate([sin8, sin8, jnp.zeros((t, rest), F32)], axis=1)
    return jnp.tile(cos_h, (1, LANES // HEAD_DIM)), jnp.tile(sin_h, (1, LANES // HEAD_DIM))


def _inproj_kernel(x_ref, sc_ref, sh_ref, g_ref, w_ref, cos_ref, sin_ref,
                   q_ref, k_ref, v_ref, a_ref, *, attn_w, conv_w):
    x = x_ref[0]
    h = _rms(x, g_ref[...]) * (1.0 + sc_ref[0]) + sh_ref[0]
    proj = jnp.dot(h.astype(BF16), w_ref[...], preferred_element_type=F32)
    cosv = cos_ref[...]
    sinv = sin_ref[...]
    lane = lax.broadcasted_iota(I32, cosv.shape, 1)
    first_half = (lane & (HEAD_DIM - 1)) < (ROPE_DIM // 2)

    def rope(t):
        up = pltpu.roll(t, LANES - ROPE_DIM // 2, 1)
        dn = pltpu.roll(t, ROPE_DIM // 2, 1)
        return t * cosv + jnp.where(first_half, -up, dn) * sinv

    scale = HEAD_DIM ** -0.5
    for hp in range(attn_w // LANES):
        lo = hp * LANES
        q_ref[0, hp] = rope(proj[:, lo:lo + LANES]) * scale
        k_ref[0, hp] = rope(proj[:, attn_w + lo:attn_w + lo + LANES])
        v_ref[0, hp] = proj[:, 2 * attn_w + lo:2 * attn_w + lo + LANES]
    cu = proj[:, 3 * attn_w:3 * attn_w + conv_w]
    cg = proj[:, 3 * attn_w + conv_w:]
    a_ref[0] = cu * jax.nn.sigmoid(cg)


def _inproj(x, sc1, sh1, g1, w_in_bf16, cos_l, sin_l, attn_w, conv_w):
    bsz, seq, d = x.shape
    tm = min(TM_INPROJ, seq)
    n = w_in_bf16.shape[1]
    nhp = attn_w // LANES
    nt = seq // tm
    qkv_shape = jax.ShapeDtypeStruct((bsz, nhp, seq, LANES), F32)
    qkv_spec = pl.BlockSpec((1, nhp, tm, LANES), lambda b, i: (b, 0, i, 0))
    vec_spec = pl.BlockSpec((1, 1, d), lambda b, i: (b, 0, 0))
    return pl.pallas_call(
        functools.partial(_inproj_kernel, attn_w=attn_w, conv_w=conv_w),
        grid=(bsz, nt),
        in_specs=[pl.BlockSpec((1, tm, d), lambda b, i: (b, i, 0)),
                  vec_spec, vec_spec,
                  pl.BlockSpec((1, d), lambda b, i: (0, 0)),
                  pl.BlockSpec((d, n), lambda b, i: (0, 0)),
                  pl.BlockSpec((tm, LANES), lambda b, i: (b * nt + i, 0)),
                  pl.BlockSpec((tm, LANES), lambda b, i: (b * nt + i, 0))],
        out_specs=[qkv_spec, qkv_spec, qkv_spec,
                   pl.BlockSpec((1, tm, conv_w), lambda b, i: (b, i, 0))],
        out_shape=[qkv_shape, qkv_shape, qkv_shape,
                   jax.ShapeDtypeStruct((bsz, seq, conv_w), F32)],
        compiler_params=_params("arbitrary", "arbitrary"),
        name="inproj",
    )(x, sc1, sh1, g1, w_in_bf16, cos_l, sin_l)


def _attn_kernel(q_ref, k_ref, v_ref, kp_ref, vp_ref, o_ref, bias_ref, ob_ref, lb_ref, *, n_back):
    blk = ATTN_BLOCK
    chunk = q_ref.shape[2]
    qi = lax.broadcasted_iota(I32, (2 * blk, 2 * blk), 0) & (blk - 1)
    kj = lax.broadcasted_iota(I32, (2 * blk, 2 * blk), 1)
    dist = blk + qi - kj
    band = (dist >= 0) & (dist <= n_back)
    kmin = jnp.where(pl.program_id(2) > 0, 0, blk)
    bias_ref[0] = jnp.where(band & (kj >= kmin), 0.0, NEG_INF)
    bias_ref[1] = jnp.where(band, 0.0, NEG_INF)
    lane = lax.broadcasted_iota(I32, (blk, LANES), 1)
    lo = lane < HEAD_DIM

    for bi, (_, dil) in enumerate(DILATED_BRANCHES):
        span = blk * dil

        def rows(start, dil=dil):
            return pl.ds(start, blk, stride=dil) if dil > 1 else pl.ds(start, blk)

        for j in range(chunk // span):
            for r in range(dil):
                base = j * span + r
                qb = q_ref[0, 0, rows(base), :].astype(BF16)
                if j == 0:
                    kprev = kp_ref[0, 0, rows(chunk - span + r), :]
                    vprev = vp_ref[0, 0, rows(chunk - span + r), :]
                else:
                    kprev = k_ref[0, 0, rows(base - span), :]
                    vprev = v_ref[0, 0, rows(base - span), :]
                kk = jnp.concatenate([kprev, k_ref[0, 0, rows(base), :]], axis=0).astype(BF16)
                vv = jnp.concatenate([vprev, v_ref[0, 0, rows(base), :]], axis=0).astype(BF16)
                zero = jnp.zeros_like(qb)
                q2 = jnp.concatenate([jnp.where(lo, qb, zero), jnp.where(lo, zero, qb)], axis=0)
                s = lax.dot_general(q2, kk, (((1,), (1,)), ((), ())), preferred_element_type=F32)
                s = s + bias_ref[0 if j == 0 else 1]
                m = jnp.max(s, axis=-1, keepdims=True)
                p = jnp.exp(s - m)
                l = jnp.sum(p, axis=-1, keepdims=True)
                pv = jnp.dot(p.astype(BF16), vv, preferred_element_type=F32)
                o2 = pv * (1.0 / l)
                lse2 = m + jnp.log(l)
                ob_ref[bi, rows(base), :] = jnp.where(lo, o2[:blk], o2[blk:])
                lb_ref[bi, rows(base), :] = jnp.where(
                    lo, jnp.broadcast_to(lse2[:blk], (blk, LANES)),
                    jnp.broadcast_to(lse2[blk:], (blk, LANES)))

    step = 2 * blk
    for c in range(chunk // step):
        sl = slice(c * step, (c + 1) * step)
        ls = [lb_ref[bi, sl, :] for bi in range(len(DILATED_BRANCHES))]
        m = functools.reduce(jnp.maximum, ls)
        es = [jnp.exp(l - m) for l in ls]
        num = functools.reduce(lambda a, b: a + b,
                               [e * ob_ref[bi, sl, :] for bi, e in enumerate(es)])
        o_ref[0, 0, sl, :] = num * (1.0 / functools.reduce(lambda a, b: a + b, es))


def _attention(q4, k4, v4):
    bsz, nhp, seq, _ = q4.shape
    n_back = DILATED_BRANCHES[0][0] // DILATED_BRANCHES[0][1]
    assert all(w // dl == n_back for w, dl in DILATED_BRANCHES) and n_back <= ATTN_BLOCK
    assert all(ATTN_CHUNK % (dl * ATTN_BLOCK) == 0 for _, dl in DILATED_BRANCHES)
    assert seq % ATTN_CHUNK == 0
    cur = pl.BlockSpec((1, 1, ATTN_CHUNK, LANES), lambda b, h, n: (b, h, n, 0))
    prev = pl.BlockSpec((1, 1, ATTN_CHUNK, LANES), lambda b, h, n: (b, h, jnp.maximum(n - 1, 0), 0))
    nbr = len(DILATED_BRANCHES)
    return pl.pallas_call(
        functools.partial(_attn_kernel, n_back=n_back),
        grid=(bsz, nhp, seq // ATTN_CHUNK),
        in_specs=[cur, cur, cur, prev, prev],
        out_specs=cur,
        out_shape=jax.ShapeDtypeStruct((bsz, nhp, seq, LANES), F32),
        scratch_shapes=[pltpu.VMEM((2, 2 * ATTN_BLOCK, 2 * ATTN_BLOCK), F32),
                        pltpu.VMEM((nbr, ATTN_CHUNK, LANES), F32),
                        pltpu.VMEM((nbr, ATTN_CHUNK, LANES), F32)],
        compiler_params=_params("arbitrary", "arbitrary", "arbitrary"),
        name="attn",
    )(q4, k4, v4, k4, v4)


def _mix_kernel(at_ref, a_ref, ap_ref, x_ref,
                gt_ref, sc_ref, sh_ref, wdw_ref, bdw_ref, lng_ref, lnb_ref, wpw_ref, bpw_ref,
                ga_ref, gc_ref, wout_ref, g2_ref, x1_ref, h2_ref, h2p_ref, acat_ref, shift_ref):
    tm = a_ref.shape[1]
    acat_ref[CONV_HALO:, :] = a_ref[0]
    halo = ap_ref[0]
    acat_ref[:CONV_HALO, :] = jnp.where(pl.program_id(1) > 0, halo, jnp.zeros_like(halo))
    sub = shift_ref.shape[0]
    keep = shift_ref.shape[1]
    for s in range(1, sub):
        shift_ref[s] = acat_ref[pl.ds(s, keep), :]
    off = CONV_HALO - (CONV_KERNEL - 1)

    def tap(j):
        a, s = divmod(off + j, sub)
        src = acat_ref if s == 0 else shift_ref.at[s]
        return src[pl.ds(a * sub, tm), :] * wdw_ref[j:j + 1, :]

    y = tap(0)
    for j in range(1, CONV_KERNEL):
        y = y + tap(j)
    y = y + bdw_ref[...]
    mu = jnp.mean(y, axis=-1, keepdims=True)
    yc = y - mu
    var = jnp.mean(yc * yc, axis=-1, keepdims=True)
    yn = yc * lax.rsqrt(var + NORM_EPS) * lng_ref[...] + lnb_ref[...]
    z = yn * jax.nn.sigmoid(yn)
    conv = jnp.dot(z.astype(BF16), wpw_ref[...], preferred_element_type=F32) + bpw_ref[...]
    nc = _rms(conv, gc_ref[...])
    attn = jnp.concatenate([at_ref[0, hp] for hp in range(at_ref.shape[1])], axis=1)
    na = _rms(attn, ga_ref[...])
    cat = jnp.concatenate([na, nc], axis=1).astype(BF16)
    mixed = jnp.dot(cat, wout_ref[...], preferred_element_type=F32)
    x1 = x_ref[0] + gt_ref[0] * mixed
    x1_ref[0] = x1
    h2 = _rms(x1, g2_ref[...]) * (1.0 + sc_ref[0]) + sh_ref[0]
    h2_ref[...] = h2
    _rows_store(h2p_ref, h2)


def _mix(attn4, a, x, gt1, sc2, sh2, w_dw, b_dw, ln_g, ln_b, w_pw2_bf16, b_pw2,
         g_attn, g_conv, w_out_bf16, g2):
    bsz, seq, d = x.shape
    cw = a.shape[-1]
    tm = min(TM_MIX, seq)
    nt = seq // tm
    nhp = attn4.shape[1]
    br_spec = pl.BlockSpec((1, nhp, tm, LANES), lambda b, i: (b, 0, i, 0))
    vec_d = pl.BlockSpec((1, 1, d), lambda b, i: (b, 0, 0))
    full = lambda arr: pl.BlockSpec(arr.shape, lambda b, i: (0,) * arr.ndim)
    row = lambda v: v.reshape(1, -1)
    halo_blocks = tm // CONV_HALO
    consts = [w_dw, row(b_dw), row(ln_g), row(ln_b), w_pw2_bf16, row(b_pw2),
              row(g_attn), row(g_conv), w_out_bf16, row(g2)]
    prow = d // LANES
    return pl.pallas_call(
        _mix_kernel,
        grid=(bsz, nt),
        in_specs=[br_spec] + [
            pl.BlockSpec((1, tm, cw), lambda b, i: (b, i, 0)),
            pl.BlockSpec((1, CONV_HALO, cw),
                         lambda b, i: (b, jnp.maximum(i * halo_blocks - 1, 0), 0)),
            pl.BlockSpec((1, tm, d), lambda b, i: (b, i, 0)),
            vec_d, vec_d, vec_d] + [full(t) for t in consts],
        out_specs=[pl.BlockSpec((1, tm, d), lambda b, i: (b, i, 0)),
                   pl.BlockSpec((tm, d), lambda b, i: (b * nt + i, 0)),
                   pl.BlockSpec((tm * prow, LANES), lambda b, i: (b * nt + i, 0))],
        out_shape=[jax.ShapeDtypeStruct((bsz, seq, d), F32),
                   jax.ShapeDtypeStruct((bsz * seq, d), F32),
                   jax.ShapeDtypeStruct((bsz * seq * prow, LANES), F32)],
        scratch_shapes=[pltpu.VMEM((tm + CONV_HALO, cw), F32),
                        pltpu.VMEM((SUBLANES, tm + CONV_HALO - SUBLANES, cw), F32)],
        compiler_params=_params("arbitrary", "arbitrary"),
        name="mix",
    )(attn4, a, a, x, gt1, sc2, sh2, *consts)


def _router_kernel(h_ref, w_ref, b_ref, idx_ref, wts_ref, cnt_ref, whi_ref, wlo_ref):
    tm = idx_ref.shape[1]

    @pl.when(pl.program_id(0) == 0)
    def _():
        w = w_ref[...]
        whi_ref[...] = w.astype(BF16)
        wlo_ref[...] = (w - w.astype(BF16).astype(F32)).astype(BF16)
        cnt_ref[...] = jnp.zeros_like(cnt_ref)

    h = h_ref[...]
    h_hi = h.astype(BF16)
    h_lo = (h - h_hi.astype(F32)).astype(BF16)
    w_hi = whi_ref[...]
    w_lo = wlo_ref[...]
    nt = (((1,), (1,)), ((), ()))
    logits = (lax.dot_general(w_hi, h_hi, nt, preferred_element_type=F32)
              + lax.dot_general(w_hi, h_lo, nt, preferred_element_type=F32)
              + lax.dot_general(w_lo, h_hi, nt, preferred_element_type=F32))
    scores = jax.nn.sigmoid(logits)
    biased = scores + b_ref[...]
    ne, tm = biased.shape
    gsz = ne // N_GROUPS
    gio = lax.broadcasted_iota(I32, (gsz, tm), 0).astype(F32)
    gscore = []
    for g in range(N_GROUPS):
        blk = biased[g * gsz:(g + 1) * gsz]
        m1 = jnp.max(blk, axis=0, keepdims=True)
        i1 = jnp.min(jnp.where(blk == m1, gio, float(gsz)), axis=0, keepdims=True)
        m2 = jnp.max(jnp.where(gio == i1, NEG_BIG, blk), axis=0, keepdims=True)
        gscore.append(m1 + m2)
    masked = []
    for g in range(N_GROUPS):
        beaten = jnp.zeros_like(gscore[g])
        for o in range(N_GROUPS):
            if o == g:
                continue
            wins = (gscore[o] >= gscore[g]) if o < g else (gscore[o] > gscore[g])
            beaten = beaten + jnp.where(wins, 1.0, 0.0)
        keep = beaten < float(TOPK_GROUPS)
        masked.append(jnp.where(keep, biased[g * gsz:(g + 1) * gsz], NEG_INF))
    cur = jnp.concatenate(masked, axis=0)
    eio = lax.broadcasted_iota(I32, (ne, tm), 0).astype(F32)
    picks, vals = [], []
    member = jnp.zeros_like(scores)
    for _ in range(TOP_K):
        m = jnp.max(cur, axis=0, keepdims=True)
        ix = jnp.min(jnp.where(cur == m, eio, float(ne)), axis=0, keepdims=True)
        hit = eio == ix
        vals.append(jnp.sum(jnp.where(hit, scores, 0.0), axis=0, keepdims=True))
        cur = jnp.where(hit, NEG_BIG, cur)
        member = jnp.where(hit, 1.0, member)
        picks.append(ix)
    total = vals[0]
    for v in vals[1:]:
        total = total + v
    for k in range(TOP_K):
        idx_ref[k:k + 1, :] = picks[k].astype(I32)
        wts_ref[k:k + 1, :] = vals[k] / total * ROUTED_SCALE

    cnt_ref[...] += jnp.dot(member.astype(BF16), jnp.ones((tm, LANES), BF16),
                            preferred_element_type=F32)


def _router(h2, w_router_t, b_router):
    ne, d = w_router_t.shape
    t = h2.shape[0]
    tm = min(TM_ROUTER, t)
    kt_spec = pl.BlockSpec((TOP_K, tm), lambda i: (0, i))
    return pl.pallas_call(
        _router_kernel,
        grid=(t // tm,),
        in_specs=[pl.BlockSpec((tm, d), lambda i: (i, 0)),
                  pl.BlockSpec((ne, d), lambda i: (0, 0)),
                  pl.BlockSpec((ne, 1), lambda i: (0, 0))],
        out_specs=[kt_spec, kt_spec, pl.BlockSpec((ne, LANES), lambda i: (0, 0))],
        out_shape=[jax.ShapeDtypeStruct((TOP_K, t), I32),
                   jax.ShapeDtypeStruct((TOP_K, t), F32),
                   jax.ShapeDtypeStruct((ne, LANES), F32)],
        scratch_shapes=[pltpu.VMEM((ne, d), BF16), pltpu.VMEM((ne, d), BF16)],
        compiler_params=_params("arbitrary"),
        name="router",
    )(h2, w_router_t, b_router.reshape(ne, 1))


def _block_plan(counts, n_assign):
    blk = MOE_ROWS
    ne = counts.shape[0]
    padded = ((counts + blk - 1) // blk) * blk
    pend = jnp.cumsum(padded)
    pstart = pend - padded
    nblk = -(-n_assign // blk) + ne
    cstart = jnp.cumsum(counts) - counts
    first_row = jnp.arange(nblk, dtype=I32) * blk
    block_e = jnp.minimum(jnp.sum(pend[None, :] <= first_row[:, None], axis=1), ne - 1)
    onehot = (block_e[:, None] == jnp.arange(ne, dtype=I32)[None, :]).astype(I32)
    pick = lambda v: jnp.sum(onehot * v[None, :], axis=1)
    within = first_row - pick(pstart)
    blk_first = jnp.minimum(pick(cstart) + within, n_assign).astype(I32)
    blk_first = jnp.pad(blk_first, (0, GATHER_AHEAD + 2), constant_values=n_assign)
    blk_valid = jnp.clip(pick(counts) - within, 0, blk).astype(I32)
    return (pstart // blk).astype(I32), (pend[-1:] // blk).astype(I32), blk_first, blk_valid


def _sorted_tags(idx):
    k, t = idx.shape
    a = k * t
    assert a <= (1 << TAG_BITS) and (N_EXPERTS << TAG_BITS) <= (1 << 31)
    keys = (idx.T.reshape(-1) << TAG_BITS) | jnp.arange(a, dtype=I32)
    order = jnp.sort(keys) & ((1 << TAG_BITS) - 1)
    token = order // k
    slot = (order % k) * t + token
    return jnp.pad(slot, (0, TAG_WINDOW)), jnp.pad(token, (0, TAG_WINDOW))


def _tile_copy(src, src_tok, dst, dst_tok, sem, rows):
    return pltpu.make_async_copy(src.at[pl.ds(src_tok * rows, rows), :],
                                 dst.at[pl.ds(dst_tok * rows, rows), :], sem)


def _experts_kernel(fb_ref, cnt_ref, nu_ref, bf_ref, bv_ref, h_hbm, stag_hbm, ttag_hbm, wg_hbm,
                    wu_hbm, wd_hbm, yk_hbm, xbuf, ybuf, xsem, ysem, stagbuf, ttagbuf, tsem,
                    wgf, wuf, wdf, wsem, wgb, wub, wdb):
    e = pl.program_id(0)
    ne = pl.num_programs(0)
    nu = nu_ref[0]
    rows = MOE_ROWS
    ring, brows = xbuf.shape[0], xbuf.shape[1]
    prow = brows // rows
    cnt = cnt_ref[e]
    first = fb_ref[e]
    nb = lax.shift_right_logical(cnt + (rows - 1), rows.bit_length() - 1)
    dump = yk_hbm.shape[0] // prow - ring * rows

    def tag_copies(g):
        ts = g % TAG_RING
        start = pl.multiple_of(bf_ref[g] & ~(TAG_TILE - 1), TAG_TILE)
        window = pl.ds(ts * TAG_WINDOW, TAG_WINDOW)
        return (pltpu.make_async_copy(stag_hbm.at[pl.ds(start, TAG_WINDOW)], stagbuf.at[window],
                                      tsem.at[ts, 0]),
                pltpu.make_async_copy(ttag_hbm.at[pl.ds(start, TAG_WINDOW)], ttagbuf.at[window],
                                      tsem.at[ts, 1]))

    def tag_base(g):
        g = jnp.maximum(g, 0)
        return (g % TAG_RING) * TAG_WINDOW + (bf_ref[g] & (TAG_TILE - 1))

    def x_rows_wait(s):
        pltpu.make_async_copy(h_hbm.at[pl.ds(0, brows), :], xbuf.at[s], xsem.at[s]).wait()

    def y_rows_wait(s):
        pltpu.make_async_copy(ybuf.at[s], yk_hbm.at[pl.ds(0, brows), :], ysem.at[s]).wait()

    def gather_rows(g, s):
        base = tag_base(g)
        for r in range(rows):
            _tile_copy(h_hbm, ttagbuf[base + r], xbuf.at[s], r, xsem.at[s],
                       prow).start(priority=r % 2)

    def send_rows(g, valid, s):
        base = tag_base(g)
        for r in range(rows):
            dst = jnp.where(r < valid, stagbuf[base + r], dump + s * rows + r)
            _tile_copy(ybuf.at[s], r, yk_hbm, dst, ysem.at[s], prow).start(priority=r % 2)

    def weight_copies(expert, s):
        return (pltpu.make_async_copy(wg_hbm.at[expert], wgf.at[s], wsem.at[s, 0]),
                pltpu.make_async_copy(wu_hbm.at[expert], wuf.at[s], wsem.at[s, 1]),
                pltpu.make_async_copy(wd_hbm.at[expert], wdf.at[s], wsem.at[s, 2]))

    @pl.when(e == 0)
    def _():
        for cp in weight_copies(0, 0):
            cp.start(priority=1)
        for g in range(GATHER_AHEAD + 1):
            for cp in tag_copies(g):
                cp.start()
        for g in range(GATHER_AHEAD):
            for cp in tag_copies(g):
                cp.wait()
            gather_rows(g, g)

    @pl.when(e + 1 < ne)
    def _():
        for cp in weight_copies(e + 1, (e + 1) % 2):
            cp.start(priority=1)

    for cp in weight_copies(e, e % 2):
        cp.wait()

    @pl.when(nb > 0)
    def _():
        wgb[...] = wgf[e % 2].astype(BF16)
        wub[...] = wuf[e % 2].astype(BF16)
        wdb[...] = wdf[e % 2].astype(BF16)

    def block(j, carry):
        g = first + j
        s = g % ring
        x_rows_wait(s)
        for cp in tag_copies(g + GATHER_AHEAD):
            cp.wait()
        for cp in tag_copies(g + GATHER_AHEAD + 1):
            cp.start()

        @pl.when(g >= ring - 1)
        def _():
            y_rows_wait(s)

        prev_valid = jnp.where(g > 0, bv_ref[jnp.maximum(g - 1, 0)], 0)

        for slot in range(ring):
            @pl.when(s == slot)
            def _(slot=slot):
                x = _rows_load(xbuf.at[slot], rows)
                live = lax.broadcasted_iota(I32, (rows, 1), 0) < cnt - j * rows
                xb = jnp.where(live, x, 0.0).astype(BF16)
                gate = jnp.dot(xb, wgb[...], preferred_element_type=F32)
                up = jnp.dot(xb, wub[...], preferred_element_type=F32)
                act = (gate * jax.nn.sigmoid(gate)) * up
                y = jnp.dot(act.astype(BF16), wdb[...], preferred_element_type=F32)
                send_rows(g - 1, prev_valid, (slot - 1) % ring)
                gather_rows(g + GATHER_AHEAD, (slot + GATHER_AHEAD) % ring)
                _rows_store(ybuf.at[slot], y)
        return carry

    lax.fori_loop(0, nb, block, 0)

    @pl.when(e == ne - 1)
    def _():
        send_rows(nu - 1, bv_ref[nu - 1], (nu - 1) % ring)
        for g in range(GATHER_AHEAD):
            x_rows_wait((nu + g) % ring)
        for cp in tag_copies(nu + GATHER_AHEAD):
            cp.wait()
        for back in range(ring, 0, -1):
            @pl.when(nu - back >= -1)
            def _(back=back):
                y_rows_wait((nu - back) % ring)


def _experts(h2_packed, slot_tags, token_tags, first_blk, counts, n_used, blk_first, blk_valid,
             n_slots, w_g, w_u, w_d):
    ne, d, f = w_g.shape
    prow = d // LANES
    brows = MOE_ROWS * prow
    assert GATHER_AHEAD + 2 <= MOE_RING and GATHER_AHEAD + 3 <= TAG_RING
    any_spec = pl.BlockSpec(memory_space=pl.ANY)
    grid_spec = pltpu.PrefetchScalarGridSpec(
        num_scalar_prefetch=5,
        grid=(ne,),
        in_specs=[any_spec] * 6,
        out_specs=any_spec,
        scratch_shapes=[pltpu.VMEM((MOE_RING, brows, LANES), F32),
                        pltpu.VMEM((MOE_RING, brows, LANES), F32),
                        pltpu.SemaphoreType.DMA((MOE_RING,)), pltpu.SemaphoreType.DMA((MOE_RING,)),
                        pltpu.SMEM((TAG_RING * TAG_WINDOW,), I32),
                        pltpu.SMEM((TAG_RING * TAG_WINDOW,), I32),
                        pltpu.SemaphoreType.DMA((TAG_RING, 2)),
                        pltpu.VMEM((2, d, f), F32), pltpu.VMEM((2, d, f), F32),
                        pltpu.VMEM((2, f, d), F32), pltpu.SemaphoreType.DMA((2, 3)),
                        pltpu.VMEM((d, f), BF16), pltpu.VMEM((d, f), BF16),
                        pltpu.VMEM((f, d), BF16)],
    )
    return pl.pallas_call(
        _experts_kernel,
        grid_spec=grid_spec,
        out_shape=jax.ShapeDtypeStruct(((n_slots + MOE_RING * MOE_ROWS) * prow, LANES), F32),
        compiler_params=_params("arbitrary"),
        name="experts",
    )(first_blk, counts, n_used, blk_first, blk_valid, h2_packed, slot_tags, token_tags,
      w_g, w_u, w_d)


def _combine_kernel(*refs):
    yk_refs = refs[:TOP_K]
    w_ref, h_ref, x_ref, gt_ref, wsg_ref, wsu_ref, wsd_ref, gf_ref, o_ref = refs[TOP_K:]
    tm = x_ref.shape[0]
    hb = _rows_load(h_ref, tm).astype(BF16)
    g = jnp.dot(hb, wsg_ref[...], preferred_element_type=F32)
    u = jnp.dot(hb, wsu_ref[...], preferred_element_type=F32)
    act = (g * jax.nn.sigmoid(g)) * u
    y = jnp.dot(act.astype(BF16), wsd_ref[...], preferred_element_type=F32)
    w = w_ref[...]
    routed = w[:, 0:1] * _rows_load(yk_refs[0], tm)
    for k in range(1, TOP_K):
        routed = routed + w[:, k:k + 1] * _rows_load(yk_refs[k], tm)
    xo = x_ref[...] + gt_ref[0] * (routed + y)
    o_ref[...] = _rms(xo, gf_ref[...])


def _combine(yk, wts_t, h2_packed, x1_flat, gt2, ws_g, ws_u, ws_d, g_final, seq):
    t, d = x1_flat.shape
    f = ws_g.shape[1]
    nchunk = d // LANES
    tm = min(TM_COMBINE, seq)
    nt = t // tm
    per_seq = seq // tm
    slot_spec = lambda k: pl.BlockSpec((tm * nchunk, LANES), lambda i: (k * nt + i, 0))
    return pl.pallas_call(
        _combine_kernel,
        grid=(nt,),
        in_specs=[slot_spec(k) for k in range(TOP_K)] + [
            pl.BlockSpec((tm, TOP_K), lambda i: (i, 0)),
            pl.BlockSpec((tm * nchunk, LANES), lambda i: (i, 0)),
            pl.BlockSpec((tm, d), lambda i: (i, 0)),
            pl.BlockSpec((1, 1, d), lambda i: (i // per_seq, 0, 0)),
            pl.BlockSpec((d, f), lambda i: (0, 0)),
            pl.BlockSpec((d, f), lambda i: (0, 0)),
            pl.BlockSpec((f, d), lambda i: (0, 0)),
            pl.BlockSpec((1, d), lambda i: (0, 0)),
        ],
        out_specs=pl.BlockSpec((tm, d), lambda i: (i, 0)),
        out_shape=jax.ShapeDtypeStruct((t, d), F32),
        compiler_params=_params("arbitrary"),
        name="combine",
    )(*([yk] * TOP_K), wts_t, h2_packed, x1_flat, gt2, ws_g, ws_u, ws_d, g_final.reshape(1, d))


def kernel(x, c, positions, g_norm1, w_ada, b_ada, w_in, w_dw, b_dw, g_conv_ln, b_conv_ln, w_pw2, b_pw2, g_attn_out, g_conv_out, w_out, g_norm2, w_router, b_router, w_exp_gate, w_exp_up, w_exp_down, w_sh_gate, w_sh_up, w_sh_down, g_final):
    bsz, seq, d = x.shape
    depth = w_ada.shape[0]
    assert depth == 1, "the final RMSNorm is fused into the single layer's combine kernel"
    attn_w = g_attn_out.shape[-1]
    conv_w = g_conv_out.shape[-1]
    cos_l, sin_l = _rope_tables(positions)
    for l in range(depth):
        ada = _ada(c, w_ada[l], b_ada[l])
        sh1, sc1, gt1, sh2, sc2, gt2 = [t.reshape(bsz, 1, d) for t in jnp.split(ada, 6, axis=-1)]
        q4, k4, v4, a = _inproj(x, sc1, sh1, g_norm1[l].reshape(1, d), w_in[l].astype(BF16),
                                cos_l, sin_l, attn_w, conv_w)
        attn4 = _attention(q4, k4, v4)
        x1, h2, h2p = _mix(attn4, a, x, gt1, sc2, sh2, w_dw[l], b_dw[l], g_conv_ln[l], b_conv_ln[l],
                      w_pw2[l].astype(BF16), b_pw2[l], g_attn_out[l], g_conv_out[l],
                      w_out[l].astype(BF16), g_norm2[l])
        idx, wts, cnt = _router(h2, w_router[l].T, b_router[l])
        counts = cnt[:, 0].astype(I32)
        first_blk, n_used, blk_first, blk_valid = _block_plan(counts, idx.size)
        slot_tags, token_tags = _sorted_tags(idx)
        yk = _experts(h2p, slot_tags, token_tags, first_blk, counts, n_used, blk_first, blk_valid,
                      idx.size, w_exp_gate[l], w_exp_up[l], w_exp_down[l])
        xf = x1.reshape(bsz * seq, d)
        out = _combine(yk, wts.T, h2p, xf, gt2, w_sh_gate[l].astype(BF16),
                       w_sh_up[l].astype(BF16), w_sh_down[l].astype(BF16), g_final, seq)
        x = out.reshape(bsz, seq, d)
    return x
```

```python
import functools

import jax
import jax.numpy as jnp
from jax import lax
from jax.experimental import pallas as pl
from jax.experimental.pallas import tpu as pltpu

F32 = jnp.float32
BF16 = jnp.bfloat16
I32 = jnp.int32

HEAD_DIM = 64
LANES = 128
SUBLANES = 8
ROPE_DIM = HEAD_DIM // 4
ROPE_THETA = 500000.0
DILATED_BRANCHES = ((128, 1), (512, 4), (2048, 16))
ATTN_BLOCK = 128
ATTN_CHUNK = 2048
CONV_KERNEL = 31
CONV_HALO = 32
N_EXPERTS = 256
TOP_K = 8
N_GROUPS = 8
TOPK_GROUPS = 4
ROUTED_SCALE = 2.5
NORM_EPS = 1e-6
NEG_INF = -1e30
NEG_BIG = -3e38

TM_INPROJ = 512
TM_MIX = 512
TM_ROUTER = 256
MOE_ROWS = 256
MOE_RING = 4
TAG_BITS = 18
TAG_TILE = 1024
TAG_WINDOW = 2 * TAG_TILE
GATHER_AHEAD = 2
TAG_RING = 8
TM_COMBINE = 512
VMEM_LIMIT = 56 * 1024 * 1024


def _params(*sem):
    return pltpu.CompilerParams(dimension_semantics=sem, vmem_limit_bytes=VMEM_LIMIT)


def _rms(x, g):
    return x * lax.rsqrt(jnp.mean(x * x, axis=-1, keepdims=True) + NORM_EPS) * g


def _rows_store(ref, val):
    rows, c = val.shape
    nchunk = c // LANES
    for j in range(nchunk):
        ref[pl.ds(j, rows, stride=nchunk), :] = val[:, j * LANES:(j + 1) * LANES]


def _rows_load(ref, rows):
    nchunk = ref.shape[0] // rows
    return jnp.concatenate([ref[pl.ds(j, rows, stride=nchunk), :] for j in range(nchunk)], axis=1)


def _ada_kernel(c_ref, w_ref, b_ref, o_ref):
    s = c_ref[...]
    s = s * jax.nn.sigmoid(s)
    o_ref[...] = jnp.dot(s, w_ref[...], precision=lax.Precision.HIGHEST,
                         preferred_element_type=F32) + b_ref[...]


def _ada(c, w_ada, b_ada):
    bsz, d = c.shape
    n = w_ada.shape[1]
    rows = 8
    cp = jnp.pad(c, ((0, rows - bsz), (0, 0)))
    tn = 1024
    out = pl.pallas_call(
        _ada_kernel,
        grid=(n // tn,),
        in_specs=[pl.BlockSpec((rows, d), lambda j: (0, 0)),
                  pl.BlockSpec((d, tn), lambda j: (0, j)),
                  pl.BlockSpec((1, tn), lambda j: (0, j))],
        out_specs=pl.BlockSpec((rows, tn), lambda j: (0, j)),
        out_shape=jax.ShapeDtypeStruct((rows, n), F32),
        compiler_params=_params("arbitrary"),
        name="ada",
    )(cp, w_ada, b_ada.reshape(1, n))
    return out[:bsz]


def _rope_kernel(pos_ref, f_ref, cos_ref, sin_ref):
    ang = pos_ref[...].astype(F32) * f_ref[...]
    cos_ref[...] = jnp.cos(ang)
    sin_ref[...] = jnp.sin(ang)


def _rope_tables(positions):
    half = ROPE_DIM // 2
    t = positions.size
    per_row = LANES // half
    inv_freq = ROPE_THETA ** (-(jnp.arange(half, dtype=F32) * 2.0) / ROPE_DIM)
    pos_rep = jnp.repeat(positions.reshape(t // per_row, per_row), half, axis=1)
    f_lane = jnp.tile(inv_freq, per_row).reshape(1, LANES)
    cos_t, sin_t = pl.pallas_call(
        _rope_kernel,
        out_shape=(jax.ShapeDtypeStruct((t // per_row, LANES), F32),) * 2,
        name="rope",
    )(pos_rep, f_lane)
    cos8 = cos_t.reshape(t, half)
    sin8 = sin_t.reshape(t, half)
    rest = HEAD_DIM - ROPE_DIM
    cos_h = jnp.concatenate([cos8, cos8, jnp.ones((t, rest), F32)], axis=1)
    sin_h = jnp.concatenate([sin8, sin8, jnp.zeros((t, rest), F32)], axis=1)
    return jnp.tile(cos_h, (1, LANES // HEAD_DIM)), jnp.tile(sin_h, (1, LANES // HEAD_DIM))


def _inproj_kernel(x_ref, sc_ref, sh_ref, g_ref, w_ref, cos_ref, sin_ref,
                   q_ref, k_ref, v_ref, a_ref, *, attn_w, conv_w):
    x = x_ref[0]
    h = _rms(x, g_ref[...]) * (1.0 + sc_ref[0]) + sh_ref[0]
    proj = jnp.dot(h.astype(BF16), w_ref[...], preferred_element_type=F32)
    cosv = cos_ref[...]
    sinv = sin_ref[...]
    lane = lax.broadcasted_iota(I32, cosv.shape, 1)
    first_half = (lane & (HEAD_DIM - 1)) < (ROPE_DIM // 2)

    def rope(t):
        up = pltpu.roll(t, LANES - ROPE_DIM // 2, 1)
        dn = pltpu.roll(t, ROPE_DIM // 2, 1)
        return t * cosv + jnp.where(first_half, -up, dn) * sinv

    scale = HEAD_DIM ** -0.5
    for hp in range(attn_w // LANES):
        lo = hp * LANES
        q_ref[0, hp] = rope(proj[:, lo:lo + LANES]) * scale
        k_ref[0, hp] = rope(proj[:, attn_w + lo:attn_w + lo + LANES])
        v_ref[0, hp] = proj[:, 2 * attn_w + lo:2 * attn_w + lo + LANES]
    cu = proj[:, 3 * attn_w:3 * attn_w + conv_w]
    cg = proj[:, 3 * attn_w + conv_w:]
    a_ref[0] = cu * jax.nn.sigmoid(cg)


def _inproj(x, sc1, sh1, g1, w_in_bf16, cos_l, sin_l, attn_w, conv_w):
    bsz, seq, d = x.shape
    tm = min(TM_INPROJ, seq)
    n = w_in_bf16.shape[1]
    nhp = attn_w // LANES
    nt = seq // tm
    qkv_shape = jax.ShapeDtypeStruct((bsz, nhp, seq, LANES), F32)
    qkv_spec = pl.BlockSpec((1, nhp, tm, LANES), lambda b, i: (b, 0, i, 0))
    vec_spec = pl.BlockSpec((1, 1, d), lambda b, i: (b, 0, 0))
    return pl.pallas_call(
        functools.partial(_inproj_kernel, attn_w=attn_w, conv_w=conv_w),
        grid=(bsz, nt),
        in_specs=[pl.BlockSpec((1, tm, d), lambda b, i: (b, i, 0)),
                  vec_spec, vec_spec,
                  pl.BlockSpec((1, d), lambda b, i: (0, 0)),
                  pl.BlockSpec((d, n), lambda b, i: (0, 0)),
                  pl.BlockSpec((tm, LANES), lambda b, i: (b * nt + i, 0)),
                  pl.BlockSpec((tm, LANES), lambda b, i: (b * nt + i, 0))],
        out_specs=[qkv_spec, qkv_spec, qkv_spec,
                   pl.BlockSpec((1, tm, conv_w), lambda b, i: (b, i, 0))],
        out_shape=[qkv_shape, qkv_shape, qkv_shape,
                   jax.ShapeDtypeStruct((bsz, seq, conv_w), F32)],
        compiler_params=_params("arbitrary", "arbitrary"),
        name="inproj",
    )(x, sc1, sh1, g1, w_in_bf16, cos_l, sin_l)


def _attn_kernel(q_ref, k_ref, v_ref, kp_ref, vp_ref, o_ref, bias_ref, ob_ref, lb_ref, *, n_back):
    blk = ATTN_BLOCK
    chunk = q_ref.shape[2]
    qi = lax.broadcasted_iota(I32, (2 * blk, 2 * blk), 0) & (blk - 1)
    kj = lax.broadcasted_iota(I32, (2 * blk, 2 * blk), 1)
    dist = blk + qi - kj
    band = (dist >= 0) & (dist <= n_back)
    kmin = jnp.where(pl.program_id(2) > 0, 0, blk)
    bias_ref[0] = jnp.where(band & (kj >= kmin), 0.0, NEG_INF)
    bias_ref[1] = jnp.where(band, 0.0, NEG_INF)
    lane = lax.broadcasted_iota(I32, (blk, LANES), 1)
    lo = lane < HEAD_DIM

    for bi, (_, dil) in enumerate(DILATED_BRANCHES):
        span = blk * dil

        def rows(start, dil=dil):
            return pl.ds(start, blk, stride=dil) if dil > 1 else pl.ds(start, blk)

        for j in range(chunk // span):
            for r in range(dil):
                base = j * span + r
                qb = q_ref[0, 0, rows(base), :].astype(BF16)
                if j == 0:
                    kprev = kp_ref[0, 0, rows(chunk - span + r), :]
                    vprev = vp_ref[0, 0, rows(chunk - span + r), :]
                else:
                    kprev = k_ref[0, 0, rows(base - span), :]
                    vprev = v_ref[0, 0, rows(base - span), :]
                kk = jnp.concatenate([kprev, k_ref[0, 0, rows(base), :]], axis=0).astype(BF16)
                vv = jnp.concatenate([vprev, v_ref[0, 0, rows(base), :]], axis=0).astype(BF16)
                zero = jnp.zeros_like(qb)
                q2 = jnp.concatenate([jnp.where(lo, qb, zero), jnp.where(lo, zero, qb)], axis=0)
                s = lax.dot_general(q2, kk, (((1,), (1,)), ((), ())), preferred_element_type=F32)
                s = s + bias_ref[0 if j == 0 else 1]
                m = jnp.max(s, axis=-1, keepdims=True)
                p = jnp.exp(s - m)
                l = jnp.sum(p, axis=-1, keepdims=True)
                pv = jnp.dot(p.astype(BF16), vv, preferred_element_type=F32)
                o2 = pv * (1.0 / l)
                lse2 = m + jnp.log(l)
                ob_ref[bi, rows(base), :] = jnp.where(lo, o2[:blk], o2[blk:])
                lb_ref[bi, rows(base), :] = jnp.where(
                    lo, jnp.broadcast_to(lse2[:blk], (blk, LANES)),
                    jnp.broadcast_to(lse2[blk:], (blk, LANES)))

    step = 2 * blk
    for c in range(chunk // step):
        sl = slice(c * step, (c + 1) * step)
        ls = [lb_ref[bi, sl, :] for bi in range(len(DILATED_BRANCHES))]
        m = functools.reduce(jnp.maximum, ls)
        es = [jnp.exp(l - m) for l in ls]
        num = functools.reduce(lambda a, b: a + b,
                               [e * ob_ref[bi, sl, :] for bi, e in enumerate(es)])
        o_ref[0, 0, sl, :] = num * (1.0 / functools.reduce(lambda a, b: a + b, es))


def _attention(q4, k4, v4):
    bsz, nhp, seq, _ = q4.shape
    n_back = DILATED_BRANCHES[0][0] // DILATED_BRANCHES[0][1]
    assert all(w // dl == n_back for w, dl in DILATED_BRANCHES) and n_back <= ATTN_BLOCK
    assert all(ATTN_CHUNK % (dl * ATTN_BLOCK) == 0 for _, dl in DILATED_BRANCHES)
    assert seq % ATTN_CHUNK == 0
    cur = pl.BlockSpec((1, 1, ATTN_CHUNK, LANES), lambda b, h, n: (b, h, n, 0))
    prev = pl.BlockSpec((1, 1, ATTN_CHUNK, LANES), lambda b, h, n: (b, h, jnp.maximum(n - 1, 0), 0))
    nbr = len(DILATED_BRANCHES)
    return pl.pallas_call(
        functools.partial(_attn_kernel, n_back=n_back),
        grid=(bsz, nhp, seq // ATTN_CHUNK),
        in_specs=[cur, cur, cur, prev, prev],
        out_specs=cur,
        out_shape=jax.ShapeDtypeStruct((bsz, nhp, seq, LANES), F32),
        scratch_shapes=[pltpu.VMEM((2, 2 * ATTN_BLOCK, 2 * ATTN_BLOCK), F32),
                        pltpu.VMEM((nbr, ATTN_CHUNK, LANES), F32),
                        pltpu.VMEM((nbr, ATTN_CHUNK, LANES), F32)],
        compiler_params=_params("arbitrary", "arbitrary", "arbitrary"),
        name="attn",
    )(q4, k4, v4, k4, v4)


def _mix_kernel(at_ref, a_ref, ap_ref, x_ref,
                gt_ref, sc_ref, sh_ref, wdw_ref, bdw_ref, lng_ref, lnb_ref, wpw_ref, bpw_ref,
                ga_ref, gc_ref, wout_ref, g2_ref, x1_ref, h2p_ref, acat_ref, shift_ref):
    tm = a_ref.shape[1]
    acat_ref[CONV_HALO:, :] = a_ref[0]
    halo = ap_ref[0]
    acat_ref[:CONV_HALO, :] = jnp.where(pl.program_id(1) > 0, halo, jnp.zeros_like(halo))
    sub = shift_ref.shape[0]
    keep = shift_ref.shape[1]
    for s in range(1, sub):
        shift_ref[s] = acat_ref[pl.ds(s, keep), :]
    off = CONV_HALO - (CONV_KERNEL - 1)

    def tap(j):
        a, s = divmod(off + j, sub)
        src = acat_ref if s == 0 else shift_ref.at[s]
        return src[pl.ds(a * sub, tm), :] * wdw_ref[j:j + 1, :]

    y = tap(0)
    for j in range(1, CONV_KERNEL):
        y = y + tap(j)
    y = y + bdw_ref[...]
    mu = jnp.mean(y, axis=-1, keepdims=True)
    yc = y - mu
    var = jnp.mean(yc * yc, axis=-1, keepdims=True)
    yn = yc * lax.rsqrt(var + NORM_EPS) * lng_ref[...] + lnb_ref[...]
    z = yn * jax.nn.sigmoid(yn)
    conv = jnp.dot(z.astype(BF16), wpw_ref[...], preferred_element_type=F32) + bpw_ref[...]
    nc = _rms(conv, gc_ref[...])
    attn = jnp.concatenate([at_ref[0, hp] for hp in range(at_ref.shape[1])], axis=1)
    na = _rms(attn, ga_ref[...])
    cat = jnp.concatenate([na, nc], axis=1).astype(BF16)
    mixed = jnp.dot(cat, wout_ref[...], preferred_element_type=F32)
    x1 = x_ref[0] + gt_ref[0] * mixed
    x1_ref[0] = x1
    h2 = _rms(x1, g2_ref[...]) * (1.0 + sc_ref[0]) + sh_ref[0]
    _rows_store(h2p_ref, h2)


def _mix(attn4, a, x, gt1, sc2, sh2, w_dw, b_dw, ln_g, ln_b, w_pw2_bf16, b_pw2,
         g_attn, g_conv, w_out_bf16, g2):
    bsz, seq, d = x.shape
    cw = a.shape[-1]
    tm = min(TM_MIX, seq)
    nt = seq // tm
    nhp = attn4.shape[1]
    br_spec = pl.BlockSpec((1, nhp, tm, LANES), lambda b, i: (b, 0, i, 0))
    vec_d = pl.BlockSpec((1, 1, d), lambda b, i: (b, 0, 0))
    full = lambda arr: pl.BlockSpec(arr.shape, lambda b, i: (0,) * arr.ndim)
    row = lambda v: v.reshape(1, -1)
    halo_blocks = tm // CONV_HALO
    consts = [w_dw, row(b_dw), row(ln_g), row(ln_b), w_pw2_bf16, row(b_pw2),
              row(g_attn), row(g_conv), w_out_bf16, row(g2)]
    prow = d // LANES
    return pl.pallas_call(
        _mix_kernel,
        grid=(bsz, nt),
        in_specs=[br_spec] + [
            pl.BlockSpec((1, tm, cw), lambda b, i: (b, i, 0)),
            pl.BlockSpec((1, CONV_HALO, cw),
                         lambda b, i: (b, jnp.maximum(i * halo_blocks - 1, 0), 0)),
            pl.BlockSpec((1, tm, d), lambda b, i: (b, i, 0)),
            vec_d, vec_d, vec_d] + [full(t) for t in consts],
        out_specs=[pl.BlockSpec((1, tm, d), lambda b, i: (b, i, 0)),
                   pl.BlockSpec((tm * prow, LANES), lambda b, i: (b * nt + i, 0))],
        out_shape=[jax.ShapeDtypeStruct((bsz, seq, d), F32),
                   jax.ShapeDtypeStruct((bsz * seq * prow, LANES), F32)],
        scratch_shapes=[pltpu.VMEM((tm + CONV_HALO, cw), F32),
                        pltpu.VMEM((SUBLANES, tm + CONV_HALO - SUBLANES, cw), F32)],
        compiler_params=_params("arbitrary", "arbitrary"),
        name="mix",
    )(attn4, a, a, x, gt1, sc2, sh2, *consts)


def _router_kernel(h_ref, w_ref, b_ref, idx_ref, wts_ref, cnt_ref, whi_ref, wlo_ref):
    tm = idx_ref.shape[1]

    @pl.when(pl.program_id(0) == 0)
    def _():
        w = w_ref[...]
        whi_ref[...] = w.astype(BF16)
        wlo_ref[...] = (w - w.astype(BF16).astype(F32)).astype(BF16)
        cnt_ref[...] = jnp.zeros_like(cnt_ref)

    h = _rows_load(h_ref, tm)
    h_hi = h.astype(BF16)
    h_lo = (h - h_hi.astype(F32)).astype(BF16)
    w_hi = whi_ref[...]
    w_lo = wlo_ref[...]
    nt = (((1,), (1,)), ((), ()))
    logits = (lax.dot_general(w_hi, h_hi, nt, preferred_element_type=F32)
              + lax.dot_general(w_hi, h_lo, nt, preferred_element_type=F32)
              + lax.dot_general(w_lo, h_hi, nt, preferred_element_type=F32))
    scores = jax.nn.sigmoid(logits)
    biased = scores + b_ref[...]
    ne, tm = biased.shape
    gsz = ne // N_GROUPS
    gio = lax.broadcasted_iota(I32, (gsz, tm), 0).astype(F32)
    gscore = []
    for g in range(N_GROUPS):
        blk = biased[g * gsz:(g + 1) * gsz]
        m1 = jnp.max(blk, axis=0, keepdims=True)
        i1 = jnp.min(jnp.where(blk == m1, gio, float(gsz)), axis=0, keepdims=True)
        m2 = jnp.max(jnp.where(gio == i1, NEG_BIG, blk), axis=0, keepdims=True)
        gscore.append(m1 + m2)
    masked = []
    for g in range(N_GROUPS):
        beaten = jnp.zeros_like(gscore[g])
        for o in range(N_GROUPS):
            if o == g:
                continue
            wins = (gscore[o] >= gscore[g]) if o < g else (gscore[o] > gscore[g])
            beaten = beaten + jnp.where(wins, 1.0, 0.0)
        keep = beaten < float(TOPK_GROUPS)
        masked.append(jnp.where(keep, biased[g * gsz:(g + 1) * gsz], NEG_INF))
    cur = jnp.concatenate(masked, axis=0)
    eio = lax.broadcasted_iota(I32, (ne, tm), 0).astype(F32)
    picks, vals = [], []
    member = jnp.zeros_like(scores)
    for _ in range(TOP_K):
        m = jnp.max(cur, axis=0, keepdims=True)
        ix = jnp.min(jnp.where(cur == m, eio, float(ne)), axis=0, keepdims=True)
        hit = eio == ix
        vals.append(jnp.sum(jnp.where(hit, scores, 0.0), axis=0, keepdims=True))
        cur = jnp.where(hit, NEG_BIG, cur)
        member = jnp.where(hit, 1.0, member)
        picks.append(ix)
    total = vals[0]
    for v in vals[1:]:
        total = total + v
    for k in range(TOP_K):
        idx_ref[k:k + 1, :] = picks[k].astype(I32)
        wts_ref[k:k + 1, :] = vals[k] / total * ROUTED_SCALE

    cnt_ref[...] += jnp.dot(member.astype(BF16), jnp.ones((tm, LANES), BF16),
                            preferred_element_type=F32)


def _router(h2_rows, w_router_t, b_router):
    ne, d = w_router_t.shape
    nchunk = d // LANES
    t = h2_rows.shape[0] // nchunk
    tm = min(TM_ROUTER, t)
    kt_spec = pl.BlockSpec((TOP_K, tm), lambda i: (0, i))
    return pl.pallas_call(
        _router_kernel,
        grid=(t // tm,),
        in_specs=[pl.BlockSpec((tm * nchunk, LANES), lambda i: (i, 0)),
                  pl.BlockSpec((ne, d), lambda i: (0, 0)),
                  pl.BlockSpec((ne, 1), lambda i: (0, 0))],
        out_specs=[kt_spec, kt_spec, pl.BlockSpec((ne, LANES), lambda i: (0, 0))],
        out_shape=[jax.ShapeDtypeStruct((TOP_K, t), I32),
                   jax.ShapeDtypeStruct((TOP_K, t), F32),
                   jax.ShapeDtypeStruct((ne, LANES), F32)],
        scratch_shapes=[pltpu.VMEM((ne, d), BF16), pltpu.VMEM((ne, d), BF16)],
        compiler_params=_params("arbitrary"),
        name="router",
    )(h2_rows, w_router_t, b_router.reshape(ne, 1))


def _block_plan(counts, n_assign):
    blk = MOE_ROWS
    ne = counts.shape[0]
    padded = ((counts + blk - 1) // blk) * blk
    pend = jnp.cumsum(padded)
    pstart = pend - padded
    nblk = -(-n_assign // blk) + ne
    cstart = jnp.cumsum(counts) - counts
    first_row = jnp.arange(nblk, dtype=I32) * blk
    block_e = jnp.minimum(jnp.sum(pend[None, :] <= first_row[:, None], axis=1), ne - 1)
    onehot = (block_e[:, None] == jnp.arange(ne, dtype=I32)[None, :]).astype(I32)
    pick = lambda v: jnp.sum(onehot * v[None, :], axis=1)
    within = first_row - pick(pstart)
    blk_first = jnp.minimum(pick(cstart) + within, n_assign).astype(I32)
    blk_first = jnp.pad(blk_first, (0, GATHER_AHEAD + 2), constant_values=n_assign)
    blk_valid = jnp.clip(pick(counts) - within, 0, blk).astype(I32)
    return (pstart // blk).astype(I32), (pend[-1:] // blk).astype(I32), blk_first, blk_valid


def _sorted_tags(idx):
    k, t = idx.shape
    a = k * t
    assert a <= (1 << TAG_BITS) and (N_EXPERTS << TAG_BITS) <= (1 << 31)
    keys = (idx.T.reshape(-1) << TAG_BITS) | jnp.arange(a, dtype=I32)
    order = jnp.sort(keys) & ((1 << TAG_BITS) - 1)
    token = order // k
    slot = (order % k) * t + token
    return jnp.pad(slot, (0, TAG_WINDOW)), jnp.pad(token, (0, TAG_WINDOW))


def _tile_copy(src, src_tok, dst, dst_tok, sem, rows):
    return pltpu.make_async_copy(src.at[pl.ds(src_tok * rows, rows), :],
                                 dst.at[pl.ds(dst_tok * rows, rows), :], sem)


def _experts_kernel(fb_ref, cnt_ref, nu_ref, bf_ref, bv_ref, h_hbm, stag_hbm, ttag_hbm, wg_hbm,
                    wu_hbm, wd_hbm, yk_hbm, xbuf, ybuf, xsem, ysem, stagbuf, ttagbuf, tsem,
                    wgf, wuf, wdf, wsem, wgb, wub, wdb):
    e = pl.program_id(0)
    ne = pl.num_programs(0)
    nu = nu_ref[0]
    rows = MOE_ROWS
    ring, brows = xbuf.shape[0], xbuf.shape[1]
    prow = brows // rows
    cnt = cnt_ref[e]
    first = fb_ref[e]
    nb = lax.shift_right_logical(cnt + (rows - 1), rows.bit_length() - 1)
    dump = yk_hbm.shape[0] // prow - ring * rows

    def tag_copies(g):
        ts = g % TAG_RING
        start = pl.multiple_of(bf_ref[g] & ~(TAG_TILE - 1), TAG_TILE)
        window = pl.ds(ts * TAG_WINDOW, TAG_WINDOW)
        return (pltpu.make_async_copy(stag_hbm.at[pl.ds(start, TAG_WINDOW)], stagbuf.at[window],
                                      tsem.at[ts, 0]),
                pltpu.make_async_copy(ttag_hbm.at[pl.ds(start, TAG_WINDOW)], ttagbuf.at[window],
                                      tsem.at[ts, 1]))

    def tag_base(g):
        g = jnp.maximum(g, 0)
        return (g % TAG_RING) * TAG_WINDOW + (bf_ref[g] & (TAG_TILE - 1))

    def x_rows_wait(s):
        pltpu.make_async_copy(h_hbm.at[pl.ds(0, brows), :], xbuf.at[s], xsem.at[s]).wait()

    def y_rows_wait(s):
        pltpu.make_async_copy(ybuf.at[s], yk_hbm.at[pl.ds(0, brows), :], ysem.at[s]).wait()

    def gather_rows(g, s):
        base = tag_base(g)
        for r in range(rows):
            _tile_copy(h_hbm, ttagbuf[base + r], xbuf.at[s], r, xsem.at[s],
                       prow).start(priority=r % 2)

    def send_rows(g, valid, s):
        base = tag_base(g)
        for r in range(rows):
            dst = jnp.where(r < valid, stagbuf[base + r], dump + s * rows + r)
            _tile_copy(ybuf.at[s], r, yk_hbm, dst, ysem.at[s], prow).start(priority=r % 2)

    def weight_copies(expert, s):
        return (pltpu.make_async_copy(wg_hbm.at[expert], wgf.at[s], wsem.at[s, 0]),
                pltpu.make_async_copy(wu_hbm.at[expert], wuf.at[s], wsem.at[s, 1]),
                pltpu.make_async_copy(wd_hbm.at[expert], wdf.at[s], wsem.at[s, 2]))

    @pl.when(e == 0)
    def _():
        for cp in weight_copies(0, 0):
            cp.start(priority=1)
        for g in range(GATHER_AHEAD + 1):
            for cp in tag_copies(g):
                cp.start()
        for g in range(GATHER_AHEAD):
            for cp in tag_copies(g):
                cp.wait()
            gather_rows(g, g)

    @pl.when(e + 1 < ne)
    def _():
        for cp in weight_copies(e + 1, (e + 1) % 2):
            cp.start(priority=1)

    for cp in weight_copies(e, e % 2):
        cp.wait()

    @pl.when(nb > 0)
    def _():
        wgb[...] = wgf[e % 2].astype(BF16)
        wub[...] = wuf[e % 2].astype(BF16)
        wdb[...] = wdf[e % 2].astype(BF16)

    def block(j, carry):
        g = first + j
        s = g % ring
        x_rows_wait(s)
        for cp in tag_copies(g + GATHER_AHEAD):
            cp.wait()
        for cp in tag_copies(g + GATHER_AHEAD + 1):
            cp.start()

        @pl.when(g >= ring - 1)
        def _():
            y_rows_wait(s)

        prev_valid = jnp.where(g > 0, bv_ref[jnp.maximum(g - 1, 0)], 0)

        for slot in range(ring):
            @pl.when(s == slot)
            def _(slot=slot):
                x = _rows_load(xbuf.at[slot], rows)
                live = lax.broadcasted_iota(I32, (rows, 1), 0) < cnt - j * rows
                xb = jnp.where(live, x, 0.0).astype(BF16)
                gate = jnp.dot(xb, wgb[...], preferred_element_type=F32)
                up = jnp.dot(xb, wub[...], preferred_element_type=F32)
                act = (gate * jax.nn.sigmoid(gate)) * up
                y = jnp.dot(act.astype(BF16), wdb[...], preferred_element_type=F32)
                send_rows(g - 1, prev_valid, (slot - 1) % ring)
                gather_rows(g + GATHER_AHEAD, (slot + GATHER_AHEAD) % ring)
                _rows_store(ybuf.at[slot], y)
        return carry

    lax.fori_loop(0, nb, block, 0)

    @pl.when(e == ne - 1)
    def _():
        send_rows(nu - 1, bv_ref[nu - 1], (nu - 1) % ring)
        for g in range(GATHER_AHEAD):
            x_rows_wait((nu + g) % ring)
        for cp in tag_copies(nu + GATHER_AHEAD):
            cp.wait()
        for back in range(ring, 0, -1):
            @pl.when(nu - back >= -1)
            def _(back=back):
                y_rows_wait((nu - back) % ring)


def _experts(h2_packed, slot_tags, token_tags, first_blk, counts, n_used, blk_first, blk_valid,
             n_slots, w_g, w_u, w_d):
    ne, d, f = w_g.shape
    prow = d // LANES
    brows = MOE_ROWS * prow
    assert GATHER_AHEAD + 2 <= MOE_RING and GATHER_AHEAD + 3 <= TAG_RING
    any_spec = pl.BlockSpec(memory_space=pl.ANY)
    grid_spec = pltpu.PrefetchScalarGridSpec(
        num_scalar_prefetch=5,
        grid=(ne,),
        in_specs=[any_spec] * 6,
        out_specs=any_spec,
        scratch_shapes=[pltpu.VMEM((MOE_RING, brows, LANES), F32),
                        pltpu.VMEM((MOE_RING, brows, LANES), F32),
                        pltpu.SemaphoreType.DMA((MOE_RING,)), pltpu.SemaphoreType.DMA((MOE_RING,)),
                        pltpu.SMEM((TAG_RING * TAG_WINDOW,), I32),
                        pltpu.SMEM((TAG_RING * TAG_WINDOW,), I32),
                        pltpu.SemaphoreType.DMA((TAG_RING, 2)),
                        pltpu.VMEM((2, d, f), F32), pltpu.VMEM((2, d, f), F32),
                        pltpu.VMEM((2, f, d), F32), pltpu.SemaphoreType.DMA((2, 3)),
                        pltpu.VMEM((d, f), BF16), pltpu.VMEM((d, f), BF16),
                        pltpu.VMEM((f, d), BF16)],
    )
    return pl.pallas_call(
        _experts_kernel,
        grid_spec=grid_spec,
        out_shape=jax.ShapeDtypeStruct(((n_slots + MOE_RING * MOE_ROWS) * prow, LANES), F32),
        compiler_params=_params("arbitrary"),
        name="experts",
    )(first_blk, counts, n_used, blk_first, blk_valid, h2_packed, slot_tags, token_tags,
      w_g, w_u, w_d)


def _combine_kernel(*refs):
    yk_refs = refs[:TOP_K]
    w_ref, h_ref, x_ref, gt_ref, wsg_ref, wsu_ref, wsd_ref, gf_ref, o_ref = refs[TOP_K:]
    tm = x_ref.shape[0]
    hb = _rows_load(h_ref, tm).astype(BF16)
    g = jnp.dot(hb, wsg_ref[...], preferred_element_type=F32)
    u = jnp.dot(hb, wsu_ref[...], preferred_element_type=F32)
    act = (g * jax.nn.sigmoid(g)) * u
    y = jnp.dot(act.astype(BF16), wsd_ref[...], preferred_element_type=F32)
    w = w_ref[...]
    routed = w[:, 0:1] * _rows_load(yk_refs[0], tm)
    for k in range(1, TOP_K):
        routed = routed + w[:, k:k + 1] * _rows_load(yk_refs[k], tm)
    xo = x_ref[...] + gt_ref[0] * (routed + y)
    o_ref[...] = _rms(xo, gf_ref[...])


def _combine(yk, wts_t, h2_packed, x1_flat, gt2, ws_g, ws_u, ws_d, g_final, seq):
    t, d = x1_flat.shape
    f = ws_g.shape[1]
    nchunk = d // LANES
    tm = min(TM_COMBINE, seq)
    nt = t // tm
    per_seq = seq // tm
    slot_spec = lambda k: pl.BlockSpec((tm * nchunk, LANES), lambda i: (k * nt + i, 0))
    return pl.pallas_call(
        _combine_kernel,
        grid=(nt,),
        in_specs=[slot_spec(k) for k in range(TOP_K)] + [
            pl.BlockSpec((tm, TOP_K), lambda i: (i, 0)),
            pl.BlockSpec((tm * nchunk, LANES), lambda i: (i, 0)),
            pl.BlockSpec((tm, d), lambda i: (i, 0)),
            pl.BlockSpec((1, 1, d), lambda i: (i // per_seq, 0, 0)),
            pl.BlockSpec((d, f), lambda i: (0, 0)),
            pl.BlockSpec((d, f), lambda i: (0, 0)),
            pl.BlockSpec((f, d), lambda i: (0, 0)),
            pl.BlockSpec((1, d), lambda i: (0, 0)),
        ],
        out_specs=pl.BlockSpec((tm, d), lambda i: (i, 0)),
        out_shape=jax.ShapeDtypeStruct((t, d), F32),
        compiler_params=_params("arbitrary"),
        name="combine",
    )(*([yk] * TOP_K), wts_t, h2_packed, x1_flat, gt2, ws_g, ws_u, ws_d, g_final.reshape(1, d))


def kernel(x, c, positions, g_norm1, w_ada, b_ada, w_in, w_dw, b_dw, g_conv_ln, b_conv_ln, w_pw2, b_pw2, g_attn_out, g_conv_out, w_out, g_norm2, w_router, b_router, w_exp_gate, w_exp_up, w_exp_down, w_sh_gate, w_sh_up, w_sh_down, g_final):
    bsz, seq, d = x.shape
    depth = w_ada.shape[0]
    assert depth == 1, "the final RMSNorm is fused into the single layer's combine kernel"
    attn_w = g_attn_out.shape[-1]
    conv_w = g_conv_out.shape[-1]
    cos_l, sin_l = _rope_tables(positions)
    for l in range(depth):
        ada = _ada(c, w_ada[l], b_ada[l])
        sh1, sc1, gt1, sh2, sc2, gt2 = [t.reshape(bsz, 1, d) for t in jnp.split(ada, 6, axis=-1)]
        q4, k4, v4, a = _inproj(x, sc1, sh1, g_norm1[l].reshape(1, d), w_in[l].astype(BF16),
                                cos_l, sin_l, attn_w, conv_w)
        attn4 = _attention(q4, k4, v4)
        x1, h2p = _mix(attn4, a, x, gt1, sc2, sh2, w_dw[l], b_dw[l], g_conv_ln[l], b_conv_ln[l],
                      w_pw2[l].astype(BF16), b_pw2[l], g_attn_out[l], g_conv_out[l],
                      w_out[l].astype(BF16), g_norm2[l])
        idx, wts, cnt = _router(h2p, w_router[l].T, b_router[l])
        counts = cnt[:, 0].astype(I32)
        first_blk, n_used, blk_first, blk_valid = _block_plan(counts, idx.size)
        slot_tags, token_tags = _sorted_tags(idx)
        yk = _experts(h2p, slot_tags, token_tags, first_blk, counts, n_used, blk_first, blk_valid,
                      idx.size, w_exp_gate[l], w_exp_up[l], w_exp_down[l])
        xf = x1.reshape(bsz * seq, d)
        out = _combine(yk, wts.T, h2p, xf, gt2, w_sh_gate[l].astype(BF16),
                       w_sh_up[l].astype(BF16), w_sh_down[l].astype(BF16), g_final, seq)
        x = out.reshape(bsz, seq, d)
    return x
```

```python
import functools

import jax
import jax.numpy as jnp
from jax import lax
from jax.experimental import pallas as pl
from jax.experimental.pallas import tpu as pltpu

F32 = jnp.float32
BF16 = jnp.bfloat16
I32 = jnp.int32

HEAD_DIM = 64
LANES = 128
SUBLANES = 8
ROPE_DIM = HEAD_DIM // 4
ROPE_THETA = 500000.0
DILATED_BRANCHES = ((128, 1), (512, 4), (2048, 16))
ATTN_BLOCK = 128
ATTN_CHUNK = 2048
CONV_KERNEL = 31
CONV_HALO = 32
N_EXPERTS = 256
TOP_K = 8
N_GROUPS = 8
TOPK_GROUPS = 4
ROUTED_SCALE = 2.5
NORM_EPS = 1e-6
NEG_INF = -1e30
NEG_BIG = -3e38

TM_INPROJ = 512
TM_MIX = 512
TM_ROUTER = 256
MOE_ROWS = 256
MOE_RING = 4
TAG_BITS = 18
TAG_TILE = 1024
TAG_WINDOW = 2 * TAG_TILE
GATHER_AHEAD = 2
TAG_RING = 8
TM_COMBINE = 512
VMEM_LIMIT = 56 * 1024 * 1024


def _params(*sem):
    return pltpu.CompilerParams(dimension_semantics=sem, vmem_limit_bytes=VMEM_LIMIT)


def _rms(x, g):
    return x * lax.rsqrt(jnp.mean(x * x, axis=-1, keepdims=True) + NORM_EPS) * g


def _rows_store(ref, val):
    rows, c = val.shape
    nchunk = c // LANES
    for j in range(nchunk):
        ref[pl.ds(j, rows, stride=nchunk), :] = val[:, j * LANES:(j + 1) * LANES]


def _rows_load(ref, rows):
    nchunk = ref.shape[0] // rows
    return jnp.concatenate([ref[pl.ds(j, rows, stride=nchunk), :] for j in range(nchunk)], axis=1)


def _ada_kernel(c_ref, w_ref, b_ref, o_ref):
    s = c_ref[...]
    s = s * jax.nn.sigmoid(s)
    o_ref[...] = jnp.dot(s, w_ref[...], precision=lax.Precision.HIGHEST,
                         preferred_element_type=F32) + b_ref[...]


def _ada(c, w_ada, b_ada):
    bsz, d = c.shape
    n = w_ada.shape[1]
    rows = 8
    cp = jnp.pad(c, ((0, rows - bsz), (0, 0)))
    tn = 1024
    out = pl.pallas_call(
        _ada_kernel,
        grid=(n // tn,),
        in_specs=[pl.BlockSpec((rows, d), lambda j: (0, 0)),
                  pl.BlockSpec((d, tn), lambda j: (0, j)),
                  pl.BlockSpec((1, tn), lambda j: (0, j))],
        out_specs=pl.BlockSpec((rows, tn), lambda j: (0, j)),
        out_shape=jax.ShapeDtypeStruct((rows, n), F32),
        compiler_params=_params("arbitrary"),
        name="ada",
    )(cp, w_ada, b_ada.reshape(1, n))
    return out[:bsz]


def _rope_kernel(pos_ref, f_ref, cos_ref, sin_ref):
    ang = pos_ref[...].astype(F32) * f_ref[...]
    cos_ref[...] = jnp.cos(ang)
    sin_ref[...] = jnp.sin(ang)


def _rope_tables(positions):
    half = ROPE_DIM // 2
    t = positions.size
    per_row = LANES // half
    inv_freq = ROPE_THETA ** (-(jnp.arange(half, dtype=F32) * 2.0) / ROPE_DIM)
    pos_rep = jnp.repeat(positions.reshape(t // per_row, per_row), half, axis=1)
    f_lane = jnp.tile(inv_freq, per_row).reshape(1, LANES)
    cos_t, sin_t = pl.pallas_call(
        _rope_kernel,
        out_shape=(jax.ShapeDtypeStruct((t // per_row, LANES), F32),) * 2,
        name="rope",
    )(pos_rep, f_lane)
    cos8 = cos_t.reshape(t, half)
    sin8 = sin_t.reshape(t, half)
    rest = HEAD_DIM - ROPE_DIM
    cos_h = jnp.concatenate([cos8, cos8, jnp.ones((t, rest), F32)], axis=1)
    sin_h = jnp.concatenate([sin8, sin8, jnp.zeros((t, rest), F32)], axis=1)
    return jnp.tile(cos_h, (1, LANES // HEAD_DIM)), jnp.tile(sin_h, (1, LANES // HEAD_DIM))


def _inproj_kernel(x_ref, sc_ref, sh_ref, g_ref, w_ref, cos_ref, sin_ref,
                   q_ref, k_ref, v_ref, a_ref, *, attn_w, conv_w):
    x = x_ref[0]
    h = _rms(x, g_ref[...]) * (1.0 + sc_ref[0]) + sh_ref[0]
    proj = jnp.dot(h.astype(BF16), w_ref[...], preferred_element_type=F32)
    cosv = cos_ref[...]
    sinv = sin_ref[...]
    lane = lax.broadcasted_iota(I32, cosv.shape, 1)
    first_half = (lane & (HEAD_DIM - 1)) < (ROPE_DIM // 2)

    def rope(t):
        up = pltpu.roll(t, LANES - ROPE_DIM // 2, 1)
        dn = pltpu.roll(t, ROPE_DIM // 2, 1)
        return t * cosv + jnp.where(first_half, -up, dn) * sinv

    scale = HEAD_DIM ** -0.5
    for hp in range(attn_w // LANES):
        lo = hp * LANES
        q_ref[0, hp] = rope(proj[:, lo:lo + LANES]) * scale
        k_ref[0, hp] = rope(proj[:, attn_w + lo:attn_w + lo + LANES])
        v_ref[0, hp] = proj[:, 2 * attn_w + lo:2 * attn_w + lo + LANES]
    cu = proj[:, 3 * attn_w:3 * attn_w + conv_w]
    cg = proj[:, 3 * attn_w + conv_w:]
    a_ref[0] = cu * jax.nn.sigmoid(cg)


def _inproj(x, sc1, sh1, g1, w_in_bf16, cos_l, sin_l, attn_w, conv_w):
    bsz, seq, d = x.shape
    tm = min(TM_INPROJ, seq)
    n = w_in_bf16.shape[1]
    nhp = attn_w // LANES
    nt = seq // tm
    qkv_shape = jax.ShapeDtypeStruct((bsz, nhp, seq, LANES), F32)
    qkv_spec = pl.BlockSpec((1, nhp, tm, LANES), lambda b, i: (b, 0, i, 0))
    vec_spec = pl.BlockSpec((1, 1, d), lambda b, i: (b, 0, 0))
    return pl.pallas_call(
        functools.partial(_inproj_kernel, attn_w=attn_w, conv_w=conv_w),
        grid=(bsz, nt),
        in_specs=[pl.BlockSpec((1, tm, d), lambda b, i: (b, i, 0)),
                  vec_spec, vec_spec,
                  pl.BlockSpec((1, d), lambda b, i: (0, 0)),
                  pl.BlockSpec((d, n), lambda b, i: (0, 0)),
                  pl.BlockSpec((tm, LANES), lambda b, i: (b * nt + i, 0)),
                  pl.BlockSpec((tm, LANES), lambda b, i: (b * nt + i, 0))],
        out_specs=[qkv_spec, qkv_spec, qkv_spec,
                   pl.BlockSpec((1, tm, conv_w), lambda b, i: (b, i, 0))],
        out_shape=[qkv_shape, qkv_shape, qkv_shape,
                   jax.ShapeDtypeStruct((bsz, seq, conv_w), F32)],
        compiler_params=_params("arbitrary", "arbitrary"),
        name="inproj",
    )(x, sc1, sh1, g1, w_in_bf16, cos_l, sin_l)


def _attn_kernel(q_ref, k_ref, v_ref, kp_ref, vp_ref, o_ref, bias_ref, ob_ref, lb_ref, *, n_back):
    blk = ATTN_BLOCK
    chunk = q_ref.shape[2]
    qi = lax.broadcasted_iota(I32, (2 * blk, 2 * blk), 0) & (blk - 1)
    kj = lax.broadcasted_iota(I32, (2 * blk, 2 * blk), 1)
    dist = blk + qi - kj
    band = (dist >= 0) & (dist <= n_back)
    kmin = jnp.where(pl.program_id(2) > 0, 0, blk)
    bias_ref[0] = jnp.where(band & (kj >= kmin), 0.0, NEG_INF)
    bias_ref[1] = jnp.where(band, 0.0, NEG_INF)
    lane = lax.broadcasted_iota(I32, (blk, LANES), 1)
    lo = lane < HEAD_DIM

    for bi, (_, dil) in enumerate(DILATED_BRANCHES):
        span = blk * dil

        def rows(start, dil=dil):
            return pl.ds(start, blk, stride=dil) if dil > 1 else pl.ds(start, blk)

        for j in range(chunk // span):
            for r in range(dil):
                base = j * span + r
                qb = q_ref[0, 0, rows(base), :].astype(BF16)
                if j == 0:
                    kprev = kp_ref[0, 0, rows(chunk - span + r), :]
                    vprev = vp_ref[0, 0, rows(chunk - span + r), :]
                else:
                    kprev = k_ref[0, 0, rows(base - span), :]
                    vprev = v_ref[0, 0, rows(base - span), :]
                kk = jnp.concatenate([kprev, k_ref[0, 0, rows(base), :]], axis=0).astype(BF16)
                vv = jnp.concatenate([vprev, v_ref[0, 0, rows(base), :]], axis=0).astype(BF16)
                zero = jnp.zeros_like(qb)
                q2 = jnp.concatenate([jnp.where(lo, qb, zero), jnp.where(lo, zero, qb)], axis=0)
                s = lax.dot_general(q2, kk, (((1,), (1,)), ((), ())), preferred_element_type=F32)
                s = s + bias_ref[0 if j == 0 else 1]
                m = jnp.max(s, axis=-1, keepdims=True)
                p = jnp.exp(s - m)
                l = jnp.sum(p, axis=-1, keepdims=True)
                pv = jnp.dot(p.astype(BF16), vv, preferred_element_type=F32)
                o2 = pv * (1.0 / l)
                lse2 = m + jnp.log(l)
                ob_ref[bi, rows(base), :] = jnp.where(lo, o2[:blk], o2[blk:])
                lb_ref[bi, rows(base), :] = jnp.where(
                    lo, jnp.broadcast_to(lse2[:blk], (blk, LANES)),
                    jnp.broadcast_to(lse2[blk:], (blk, LANES)))

    step = 2 * blk
    for c in range(chunk // step):
        sl = slice(c * step, (c + 1) * step)
        ls = [lb_ref[bi, sl, :] for bi in range(len(DILATED_BRANCHES))]
        m = functools.reduce(jnp.maximum, ls)
        es = [jnp.exp(l - m) for l in ls]
        num = functools.reduce(lambda a, b: a + b,
                               [e * ob_ref[bi, sl, :] for bi, e in enumerate(es)])
        o_ref[0, 0, sl, :] = num * (1.0 / functools.reduce(lambda a, b: a + b, es))


def _attention(q4, k4, v4):
    bsz, nhp, seq, _ = q4.shape
    n_back = DILATED_BRANCHES[0][0] // DILATED_BRANCHES[0][1]
    assert all(w // dl == n_back for w, dl in DILATED_BRANCHES) and n_back <= ATTN_BLOCK
    assert all(ATTN_CHUNK % (dl * ATTN_BLOCK) == 0 for _, dl in DILATED_BRANCHES)
    assert seq % ATTN_CHUNK == 0
    cur = pl.BlockSpec((1, 1, ATTN_CHUNK, LANES), lambda b, h, n: (b, h, n, 0))
    prev = pl.BlockSpec((1, 1, ATTN_CHUNK, LANES), lambda b, h, n: (b, h, jnp.maximum(n - 1, 0), 0))
    nbr = len(DILATED_BRANCHES)
    return pl.pallas_call(
        functools.partial(_attn_kernel, n_back=n_back),
        grid=(bsz, nhp, seq // ATTN_CHUNK),
        in_specs=[cur, cur, cur, prev, prev],
        out_specs=cur,
        out_shape=jax.ShapeDtypeStruct((bsz, nhp, seq, LANES), F32),
        scratch_shapes=[pltpu.VMEM((2, 2 * ATTN_BLOCK, 2 * ATTN_BLOCK), F32),
                        pltpu.VMEM((nbr, ATTN_CHUNK, LANES), F32),
                        pltpu.VMEM((nbr, ATTN_CHUNK, LANES), F32)],
        compiler_params=_params("arbitrary", "arbitrary", "arbitrary"),
        name="attn",
    )(q4, k4, v4, k4, v4)


def _mix_kernel(at_ref, a_ref, ap_ref, x_ref,
                gt_ref, sc_ref, sh_ref, wdw_ref, bdw_ref, lng_ref, lnb_ref, wpw_ref, bpw_ref,
                ga_ref, gc_ref, wout_ref, g2_ref, x1_ref, h2p_ref, acat_ref, shift_ref):
    tm = a_ref.shape[1]
    acat_ref[CONV_HALO:, :] = a_ref[0]
    halo = ap_ref[0]
    acat_ref[:CONV_HALO, :] = jnp.where(pl.program_id(1) > 0, halo, jnp.zeros_like(halo))
    sub = shift_ref.shape[0]
    keep = shift_ref.shape[1]
    for s in range(1, sub):
        shift_ref[s] = acat_ref[pl.ds(s, keep), :]
    off = CONV_HALO - (CONV_KERNEL - 1)

    def tap(j):
        a, s = divmod(off + j, sub)
        src = acat_ref if s == 0 else shift_ref.at[s]
        return src[pl.ds(a * sub, tm), :] * wdw_ref[j:j + 1, :]

    y = tap(0)
    for j in range(1, CONV_KERNEL):
        y = y + tap(j)
    y = y + bdw_ref[...]
    mu = jnp.mean(y, axis=-1, keepdims=True)
    yc = y - mu
    var = jnp.mean(yc * yc, axis=-1, keepdims=True)
    yn = yc * lax.rsqrt(var + NORM_EPS) * lng_ref[...] + lnb_ref[...]
    z = yn * jax.nn.sigmoid(yn)
    conv = jnp.dot(z.astype(BF16), wpw_ref[...], preferred_element_type=F32) + bpw_ref[...]
    nc = _rms(conv, gc_ref[...])
    attn = jnp.concatenate([at_ref[0, hp] for hp in range(at_ref.shape[1])], axis=1)
    na = _rms(attn, ga_ref[...])
    cat = jnp.concatenate([na, nc], axis=1).astype(BF16)
    mixed = jnp.dot(cat, wout_ref[...], preferred_element_type=F32)
    x1 = x_ref[0] + gt_ref[0] * mixed
    x1_ref[0] = x1
    h2 = _rms(x1, g2_ref[...]) * (1.0 + sc_ref[0]) + sh_ref[0]
    _rows_store(h2p_ref, h2)


def _mix(attn4, a, x, gt1, sc2, sh2, w_dw, b_dw, ln_g, ln_b, w_pw2_bf16, b_pw2,
         g_attn, g_conv, w_out_bf16, g2):
    bsz, seq, d = x.shape
    cw = a.shape[-1]
    tm = min(TM_MIX, seq)
    nt = seq // tm
    nhp = attn4.shape[1]
    br_spec = pl.BlockSpec((1, nhp, tm, LANES), lambda b, i: (b, 0, i, 0))
    vec_d = pl.BlockSpec((1, 1, d), lambda b, i: (b, 0, 0))
    full = lambda arr: pl.BlockSpec(arr.shape, lambda b, i: (0,) * arr.ndim)
    row = lambda v: v.reshape(1, -1)
    halo_blocks = tm // CONV_HALO
    consts = [w_dw, row(b_dw), row(ln_g), row(ln_b), w_pw2_bf16, row(b_pw2),
              row(g_attn), row(g_conv), w_out_bf16, row(g2)]
    prow = d // LANES
    return pl.pallas_call(
        _mix_kernel,
        grid=(bsz, nt),
        in_specs=[br_spec] + [
            pl.BlockSpec((1, tm, cw), lambda b, i: (b, i, 0)),
            pl.BlockSpec((1, CONV_HALO, cw),
                         lambda b, i: (b, jnp.maximum(i * halo_blocks - 1, 0), 0)),
            pl.BlockSpec((1, tm, d), lambda b, i: (b, i, 0)),
            vec_d, vec_d, vec_d] + [full(t) for t in consts],
        out_specs=[pl.BlockSpec((1, tm, d), lambda b, i: (b, i, 0)),
                   pl.BlockSpec((tm * prow, LANES), lambda b, i: (b * nt + i, 0))],
        out_shape=[jax.ShapeDtypeStruct((bsz, seq, d), F32),
                   jax.ShapeDtypeStruct((bsz * seq * prow, LANES), F32)],
        scratch_shapes=[pltpu.VMEM((tm + CONV_HALO, cw), F32),
                        pltpu.VMEM((SUBLANES, tm + CONV_HALO - SUBLANES, cw), F32)],
        compiler_params=_params("arbitrary", "arbitrary"),
        name="mix",
    )(attn4, a, a, x, gt1, sc2, sh2, *consts)


def _router_kernel(h_ref, w_ref, b_ref, idx_ref, wts_ref, cnt_ref, whi_ref, wlo_ref):
    tm = idx_ref.shape[1]

    @pl.when(pl.program_id(0) == 0)
    def _():
        w = w_ref[...]
        whi_ref[...] = w.astype(BF16)
        wlo_ref[...] = (w - w.astype(BF16).astype(F32)).astype(BF16)
        cnt_ref[...] = jnp.zeros_like(cnt_ref)

    h = _rows_load(h_ref, tm)
    h_hi = h.astype(BF16)
    h_lo = (h - h_hi.astype(F32)).astype(BF16)
    w_hi = whi_ref[...]
    w_lo = wlo_ref[...]
    nt = (((1,), (1,)), ((), ()))
    logits = (lax.dot_general(w_hi, h_hi, nt, preferred_element_type=F32)
              + lax.dot_general(w_hi, h_lo, nt, preferred_element_type=F32)
              + lax.dot_general(w_lo, h_hi, nt, preferred_element_type=F32))
    scores = jax.nn.sigmoid(logits)
    biased = scores + b_ref[...]
    ne, tm = biased.shape
    gsz = ne // N_GROUPS
    gio = lax.broadcasted_iota(I32, (gsz, tm), 0).astype(F32)
    gscore = []
    for g in range(N_GROUPS):
        blk = biased[g * gsz:(g + 1) * gsz]
        m1 = jnp.max(blk, axis=0, keepdims=True)
        i1 = jnp.min(jnp.where(blk == m1, gio, float(gsz)), axis=0, keepdims=True)
        m2 = jnp.max(jnp.where(gio == i1, NEG_BIG, blk), axis=0, keepdims=True)
        gscore.append(m1 + m2)
    masked = []
    for g in range(N_GROUPS):
        beaten = jnp.zeros_like(gscore[g])
        for o in range(N_GROUPS):
            if o == g:
                continue
            wins = (gscore[o] >= gscore[g]) if o < g else (gscore[o] > gscore[g])
            beaten = beaten + jnp.where(wins, 1.0, 0.0)
        keep = beaten < float(TOPK_GROUPS)
        masked.append(jnp.where(keep, biased[g * gsz:(g + 1) * gsz], NEG_INF))
    cur = jnp.concatenate(masked, axis=0)
    eio = lax.broadcasted_iota(I32, (ne, tm), 0).astype(F32)
    picks, vals = [], []
    member = jnp.zeros_like(scores)
    for _ in range(TOP_K):
        m = jnp.max(cur, axis=0, keepdims=True)
        ix = jnp.min(jnp.where(cur == m, eio, float(ne)), axis=0, keepdims=True)
        hit = eio == ix
        vals.append(jnp.sum(jnp.where(hit, scores, 0.0), axis=0, keepdims=True))
        cur = jnp.where(hit, NEG_BIG, cur)
        member = jnp.where(hit, 1.0, member)
        picks.append(ix)
    total = vals[0]
    for v in vals[1:]:
        total = total + v
    for k in range(TOP_K):
        idx_ref[k:k + 1, :] = picks[k].astype(I32)
        wts_ref[k:k + 1, :] = vals[k] / total * ROUTED_SCALE

    cnt_ref[...] += jnp.dot(member.astype(BF16), jnp.ones((tm, LANES), BF16),
                            preferred_element_type=F32)


def _router(h2_rows, w_router_t, b_router):
    ne, d = w_router_t.shape
    nchunk = d // LANES
    t = h2_rows.shape[0] // nchunk
    tm = min(TM_ROUTER, t)
    kt_spec = pl.BlockSpec((TOP_K, tm), lambda i: (0, i))
    return pl.pallas_call(
        _router_kernel,
        grid=(t // tm,),
        in_specs=[pl.BlockSpec((tm * nchunk, LANES), lambda i: (i, 0)),
                  pl.BlockSpec((ne, d), lambda i: (0, 0)),
                  pl.BlockSpec((ne, 1), lambda i: (0, 0))],
        out_specs=[kt_spec, kt_spec, pl.BlockSpec((ne, LANES), lambda i: (0, 0))],
        out_shape=[jax.ShapeDtypeStruct((TOP_K, t), I32),
                   jax.ShapeDtypeStruct((TOP_K, t), F32),
                   jax.ShapeDtypeStruct((ne, LANES), F32)],
        scratch_shapes=[pltpu.VMEM((ne, d), BF16), pltpu.VMEM((ne, d), BF16)],
        compiler_params=_params("arbitrary"),
        name="router",
    )(h2_rows, w_router_t, b_router.reshape(ne, 1))


def _block_plan(counts, n_assign):
    blk = MOE_ROWS
    ne = counts.shape[0]
    padded = ((counts + blk - 1) // blk) * blk
    pend = jnp.cumsum(padded)
    pstart = pend - padded
    nblk = -(-n_assign // blk) + ne
    cstart = jnp.cumsum(counts) - counts
    first_row = jnp.arange(nblk, dtype=I32) * blk
    block_e = jnp.minimum(jnp.sum(pend[None, :] <= first_row[:, None], axis=1), ne - 1)
    onehot = (block_e[:, None] == jnp.arange(ne, dtype=I32)[None, :]).astype(I32)
    pick = lambda v: jnp.sum(onehot * v[None, :], axis=1)
    within = first_row - pick(pstart)
    blk_first = jnp.minimum(pick(cstart) + within, n_assign).astype(I32)
    blk_first = jnp.pad(blk_first, (0, GATHER_AHEAD + 2), constant_values=n_assign)
    blk_valid = jnp.clip(pick(counts) - within, 0, blk).astype(I32)
    return (pstart // blk).astype(I32), (pend[-1:] // blk).astype(I32), blk_first, blk_valid


def _sorted_tags(idx):
    k, t = idx.shape
    a = k * t
    assert a <= (1 << TAG_BITS) and (N_EXPERTS << TAG_BITS) <= (1 << 31)
    keys = (idx.T.reshape(-1) << TAG_BITS) | jnp.arange(a, dtype=I32)
    order = jnp.sort(keys) & ((1 << TAG_BITS) - 1)
    token = order // k
    slot = (order % k) * t + token
    return jnp.pad(slot, (0, TAG_WINDOW)), jnp.pad(token, (0, TAG_WINDOW))


def _tile_copy(src, src_tok, dst, dst_tok, sem, rows):
    return pltpu.make_async_copy(src.at[pl.ds(src_tok * rows, rows), :],
                                 dst.at[pl.ds(dst_tok * rows, rows), :], sem)


def _experts_kernel(fb_ref, cnt_ref, nu_ref, bf_ref, bv_ref, h_hbm, stag_hbm, ttag_hbm, wg_hbm,
                    wu_hbm, wd_hbm, yk_hbm, xbuf, ybuf, xsem, ysem, stagbuf, ttagbuf, tsem,
                    wgf, wuf, wdf, wsem, wgb, wub, wdb):
    e = pl.program_id(0)
    ne = pl.num_programs(0)
    nu = nu_ref[0]
    rows = MOE_ROWS
    ring, brows = xbuf.shape[0], xbuf.shape[1]
    prow = brows // rows
    cnt = cnt_ref[e]
    first = fb_ref[e]
    nb = lax.shift_right_logical(cnt + (rows - 1), rows.bit_length() - 1)
    dump = yk_hbm.shape[0] // prow - ring * rows

    def tag_copies(g):
        ts = g % TAG_RING
        start = pl.multiple_of(bf_ref[g] & ~(TAG_TILE - 1), TAG_TILE)
        window = pl.ds(ts * TAG_WINDOW, TAG_WINDOW)
        return (pltpu.make_async_copy(stag_hbm.at[pl.ds(start, TAG_WINDOW)], stagbuf.at[window],
                                      tsem.at[ts, 0]),
                pltpu.make_async_copy(ttag_hbm.at[pl.ds(start, TAG_WINDOW)], ttagbuf.at[window],
                                      tsem.at[ts, 1]))

    def tag_base(g):
        g = jnp.maximum(g, 0)
        return (g % TAG_RING) * TAG_WINDOW + (bf_ref[g] & (TAG_TILE - 1))

    def x_rows_wait(s):
        pltpu.make_async_copy(h_hbm.at[pl.ds(0, brows), :], xbuf.at[s], xsem.at[s]).wait()

    def y_rows_wait(s):
        pltpu.make_async_copy(ybuf.at[s], yk_hbm.at[pl.ds(0, brows), :], ysem.at[s]).wait()

    def gather_rows(g, s):
        base = tag_base(g)
        for r in range(rows):
            _tile_copy(h_hbm, ttagbuf[base + r], xbuf.at[s], r, xsem.at[s],
                       prow).start(priority=0)

    def send_rows(g, valid, s):
        base = tag_base(g)
        for r in range(rows):
            dst = jnp.where(r < valid, stagbuf[base + r], dump + s * rows + r)
            _tile_copy(ybuf.at[s], r, yk_hbm, dst, ysem.at[s], prow).start(priority=1)

    def weight_copies(expert, s):
        return (pltpu.make_async_copy(wg_hbm.at[expert], wgf.at[s], wsem.at[s, 0]),
                pltpu.make_async_copy(wu_hbm.at[expert], wuf.at[s], wsem.at[s, 1]),
                pltpu.make_async_copy(wd_hbm.at[expert], wdf.at[s], wsem.at[s, 2]))

    @pl.when(e == 0)
    def _():
        for cp in weight_copies(0, 0):
            cp.start(priority=1)
        for g in range(GATHER_AHEAD + 1):
            for cp in tag_copies(g):
                cp.start()
        for g in range(GATHER_AHEAD):
            for cp in tag_copies(g):
                cp.wait()
            gather_rows(g, g)

    @pl.when(e + 1 < ne)
    def _():
        for cp in weight_copies(e + 1, (e + 1) % 2):
            cp.start(priority=1)

    for cp in weight_copies(e, e % 2):
        cp.wait()

    @pl.when(nb > 0)
    def _():
        wgb[...] = wgf[e % 2].astype(BF16)
        wub[...] = wuf[e % 2].astype(BF16)
        wdb[...] = wdf[e % 2].astype(BF16)

    def block(j, carry):
        g = first + j
        s = g % ring
        x_rows_wait(s)
        for cp in tag_copies(g + GATHER_AHEAD):
            cp.wait()
        for cp in tag_copies(g + GATHER_AHEAD + 1):
            cp.start()

        @pl.when(g >= ring - 1)
        def _():
            y_rows_wait(s)

        prev_valid = jnp.where(g > 0, bv_ref[jnp.maximum(g - 1, 0)], 0)

        for slot in range(ring):
            @pl.when(s == slot)
            def _(slot=slot):
                x = _rows_load(xbuf.at[slot], rows)
                live = lax.broadcasted_iota(I32, (rows, 1), 0) < cnt - j * rows
                xb = jnp.where(live, x, 0.0).astype(BF16)
                gate = jnp.dot(xb, wgb[...], preferred_element_type=F32)
                up = jnp.dot(xb, wub[...], preferred_element_type=F32)
                act = (gate * jax.nn.sigmoid(gate)) * up
                y = jnp.dot(act.astype(BF16), wdb[...], preferred_element_type=F32)
                send_rows(g - 1, prev_valid, (slot - 1) % ring)
                gather_rows(g + GATHER_AHEAD, (slot + GATHER_AHEAD) % ring)
                _rows_store(ybuf.at[slot], y)
        return carry

    lax.fori_loop(0, nb, block, 0)

    @pl.when(e == ne - 1)
    def _():
        send_rows(nu - 1, bv_ref[nu - 1], (nu - 1) % ring)
        for g in range(GATHER_AHEAD):
            x_rows_wait((nu + g) % ring)
        for cp in tag_copies(nu + GATHER_AHEAD):
            cp.wait()
        for back in range(ring, 0, -1):
            @pl.when(nu - back >= -1)
            def _(back=back):
                y_rows_wait((nu - back) % ring)


def _experts(h2_packed, slot_tags, token_tags, first_blk, counts, n_used, blk_first, blk_valid,
             n_slots, w_g, w_u, w_d):
    ne, d, f = w_g.shape
    prow = d // LANES
    brows = MOE_ROWS * prow
    assert GATHER_AHEAD + 2 <= MOE_RING and GATHER_AHEAD + 3 <= TAG_RING
    any_spec = pl.BlockSpec(memory_space=pl.ANY)
    grid_spec = pltpu.PrefetchScalarGridSpec(
        num_scalar_prefetch=5,
        grid=(ne,),
        in_specs=[any_spec] * 6,
        out_specs=any_spec,
        scratch_shapes=[pltpu.VMEM((MOE_RING, brows, LANES), F32),
                        pltpu.VMEM((MOE_RING, brows, LANES), F32),
                        pltpu.SemaphoreType.DMA((MOE_RING,)), pltpu.SemaphoreType.DMA((MOE_RING,)),
                        pltpu.SMEM((TAG_RING * TAG_WINDOW,), I32),
                        pltpu.SMEM((TAG_RING * TAG_WINDOW,), I32),
                        pltpu.SemaphoreType.DMA((TAG_RING, 2)),
                        pltpu.VMEM((2, d, f), F32), pltpu.VMEM((2, d, f), F32),
                        pltpu.VMEM((2, f, d), F32), pltpu.SemaphoreType.DMA((2, 3)),
                        pltpu.VMEM((d, f), BF16), pltpu.VMEM((d, f), BF16),
                        pltpu.VMEM((f, d), BF16)],
    )
    return pl.pallas_call(
        _experts_kernel,
        grid_spec=grid_spec,
        out_shape=jax.ShapeDtypeStruct(((n_slots + MOE_RING * MOE_ROWS) * prow, LANES), F32),
        compiler_params=_params("arbitrary"),
        name="experts",
    )(first_blk, counts, n_used, blk_first, blk_valid, h2_packed, slot_tags, token_tags,
      w_g, w_u, w_d)


def _combine_kernel(*refs):
    yk_refs = refs[:TOP_K]
    w_ref, h_ref, x_ref, gt_ref, wsg_ref, wsu_ref, wsd_ref, gf_ref, o_ref = refs[TOP_K:]
    tm = x_ref.shape[0]
    hb = _rows_load(h_ref, tm).astype(BF16)
    g = jnp.dot(hb, wsg_ref[...], preferred_element_type=F32)
    u = jnp.dot(hb, wsu_ref[...], preferred_element_type=F32)
    act = (g * jax.nn.sigmoid(g)) * u
    y = jnp.dot(act.astype(BF16), wsd_ref[...], preferred_element_type=F32)
    w = w_ref[...]
    routed = w[:, 0:1] * _rows_load(yk_refs[0], tm)
    for k in range(1, TOP_K):
        routed = routed + w[:, k:k + 1] * _rows_load(yk_refs[k], tm)
    xo = x_ref[...] + gt_ref[0] * (routed + y)
    o_ref[...] = _rms(xo, gf_ref[...])


def _combine(yk, wts_t, h2_packed, x1_flat, gt2, ws_g, ws_u, ws_d, g_final, seq):
    t, d = x1_flat.shape
    f = ws_g.shape[1]
    nchunk = d // LANES
    tm = min(TM_COMBINE, seq)
    nt = t // tm
    per_seq = seq // tm
    slot_spec = lambda k: pl.BlockSpec((tm * nchunk, LANES), lambda i: (k * nt + i, 0))
    return pl.pallas_call(
        _combine_kernel,
        grid=(nt,),
        in_specs=[slot_spec(k) for k in range(TOP_K)] + [
            pl.BlockSpec((tm, TOP_K), lambda i: (i, 0)),
            pl.BlockSpec((tm * nchunk, LANES), lambda i: (i, 0)),
            pl.BlockSpec((tm, d), lambda i: (i, 0)),
            pl.BlockSpec((1, 1, d), lambda i: (i // per_seq, 0, 0)),
            pl.BlockSpec((d, f), lambda i: (0, 0)),
            pl.BlockSpec((d, f), lambda i: (0, 0)),
            pl.BlockSpec((f, d), lambda i: (0, 0)),
            pl.BlockSpec((1, d), lambda i: (0, 0)),
        ],
        out_specs=pl.BlockSpec((tm, d), lambda i: (i, 0)),
        out_shape=jax.ShapeDtypeStruct((t, d), F32),
        compiler_params=_params("arbitrary"),
        name="combine",
    )(*([yk] * TOP_K), wts_t, h2_packed, x1_flat, gt2, ws_g, ws_u, ws_d, g_final.reshape(1, d))


def kernel(x, c, positions, g_norm1, w_ada, b_ada, w_in, w_dw, b_dw, g_conv_ln, b_conv_ln, w_pw2, b_pw2, g_attn_out, g_conv_out, w_out, g_norm2, w_router, b_router, w_exp_gate, w_exp_up, w_exp_down, w_sh_gate, w_sh_up, w_sh_down, g_final):
    bsz, seq, d = x.shape
    depth = w_ada.shape[0]
    assert depth == 1, "the final RMSNorm is fused into the single layer's combine kernel"
    attn_w = g_attn_out.shape[-1]
    conv_w = g_conv_out.shape[-1]
    cos_l, sin_l = _rope_tables(positions)
    for l in range(depth):
        ada = _ada(c, w_ada[l], b_ada[l])
        sh1, sc1, gt1, sh2, sc2, gt2 = [t.reshape(bsz, 1, d) for t in jnp.split(ada, 6, axis=-1)]
        q4, k4, v4, a = _inproj(x, sc1, sh1, g_norm1[l].reshape(1, d), w_in[l].astype(BF16),
                                cos_l, sin_l, attn_w, conv_w)
        attn4 = _attention(q4, k4, v4)
        x1, h2p = _mix(attn4, a, x, gt1, sc2, sh2, w_dw[l], b_dw[l], g_conv_ln[l], b_conv_ln[l],
                      w_pw2[l].astype(BF16), b_pw2[l], g_attn_out[l], g_conv_out[l],
                      w_out[l].astype(BF16), g_norm2[l])
        idx, wts, cnt = _router(h2p, w_router[l].T, b_router[l])
        counts = cnt[:, 0].astype(I32)
        first_blk, n_used, blk_first, blk_valid = _block_plan(counts, idx.size)
        slot_tags, token_tags = _sorted_tags(idx)
        yk = _experts(h2p, slot_tags, token_tags, first_blk, counts, n_used, blk_first, blk_valid,
                      idx.size, w_exp_gate[l], w_exp_up[l], w_exp_down[l])
        xf = x1.reshape(bsz * seq, d)
        out = _combine(yk, wts.T, h2p, xf, gt2, w_sh_gate[l].astype(BF16),
                       w_sh_up[l].astype(BF16), w_sh_down[l].astype(BF16), g_final, seq)
        x = out.reshape(bsz, seq, d)
    return x
```
